```python
import math
import numpy as np
import jax
import jax.numpy as jnp
from jax import lax

D_MODEL = 1024
BATCH = 2
SEQ = 8192
DEPTH = 2

DA_HEADS = 4
DA_HEAD_DIM = 64
DA_WIDTH = DA_HEADS * 2 * DA_HEAD_DIM
Q_BLOCK = 128
ROPE_THETA = 10000.0
DN_HEADS = 4
DN_HEAD_DIM = 128
DN_WIDTH = DN_HEADS * DN_HEAD_DIM
CONV_K = 5
CHUNK = 64
D_FF = 4 * D_MODEL
EPS = 1e-6
IN_SPLITS = (DA_WIDTH, DA_WIDTH, DA_WIDTH, DN_WIDTH, DN_WIDTH, DN_WIDTH, DN_WIDTH,
             2 * DN_HEADS, 2 * DN_HEADS, 2 * D_MODEL)
IN_COLS = sum(IN_SPLITS)

kernel_name = 'hybrid_diffattn_gdn_encoder'


def rms_norm(x, gain):
    x32 = x.astype(jnp.float32)
    y = x32 * lax.rsqrt(jnp.mean(x32 * x32, axis=-1, keepdims=True) + EPS)
    return (y * gain.astype(jnp.float32)).astype(x.dtype)


def l2_norm(x):
    x32 = x.astype(jnp.float32)
    return (x32 * lax.rsqrt(jnp.sum(x32 * x32, axis=-1, keepdims=True) + EPS)).astype(x.dtype)


def rope_tables(seq, dim, dtype):
    pos = jnp.arange(seq, dtype=jnp.float32)
    inv = 1.0 / (ROPE_THETA ** (jnp.arange(0, dim, 2, dtype=jnp.float32) / dim))
    ang = pos[:, None] * inv[None, :]
    ang = jnp.concatenate([ang, ang], axis=-1)
    return jnp.cos(ang).astype(dtype), jnp.sin(ang).astype(dtype)


def apply_rope(x, cos, sin):
    half = x.shape[-1] // 2
    rot = jnp.concatenate([-x[..., half:], x[..., :half]], axis=-1)
    return x * cos + rot * sin


def diff_attention(q, k, v, lam, lambda_init, cos, sin, qn_gain, kn_gain, subln_gain):
    B, S, H, _, d = q.shape
    q = apply_rope(rms_norm(q, qn_gain).transpose(0, 2, 3, 1, 4), cos, sin)
    k = apply_rope(rms_norm(k, kn_gain).transpose(0, 2, 3, 1, 4), cos, sin)
    v = v.transpose(0, 2, 1, 3)
    nb = S // Q_BLOCK
    qb = q.reshape(B, H, 2, nb, Q_BLOCK, d).transpose(3, 0, 1, 2, 4, 5)
    scale = d ** -0.5

    def block(qi):
        s = jnp.einsum('bhtqd,bhtkd->bhtqk', qi, k).astype(jnp.float32) * scale
        p = jax.nn.softmax(s, axis=-1)
        a = p[:, :, 0] - lam * p[:, :, 1]
        return jnp.einsum('bhqk,bhkd->bhqd', a.astype(v.dtype), v)

    o = lax.map(block, qb)
    o = o.transpose(1, 0, 3, 2, 4).reshape(B, S, H, 2 * d)
    o = rms_norm(o, subln_gain) * (1.0 - lambda_init)
    return o.reshape(B, S, H * 2 * d)


def gated_delta_rule_chunked(q, k, v, g, beta):
    out_dtype = v.dtype
    q, k, v, g, beta = [t.astype(jnp.float32) for t in (q, k, v, g, beta)]
    B, H, S, Dk = q.shape
    Dv = v.shape[-1]
    n = S // CHUNK
    q = q.reshape(B, H, n, CHUNK, Dk)
    k = k.reshape(B, H, n, CHUNK, Dk)
    v = v.reshape(B, H, n, CHUNK, Dv)
    beta = beta.reshape(B, H, n, CHUNK)
    g = jnp.cumsum(g.reshape(B, H, n, CHUNK), axis=-1)
    idx = jnp.arange(CHUNK)
    incl = idx[:, None] >= idx[None, :]
    strict = idx[:, None] > idx[None, :]
    gdiff = g[..., :, None] - g[..., None, :]
    decay = jnp.where(incl, jnp.exp(jnp.where(incl, gdiff, 0.0)), 0.0)
    kb = k * beta[..., None]
    L = jnp.where(strict, jnp.einsum('bhncd,bhnmd->bhncm', kb, k) * decay, 0.0)
    eye = jnp.eye(CHUNK, dtype=jnp.float32)
    T = lax.linalg.triangular_solve(eye + L, jnp.broadcast_to(eye, L.shape), left_side=True,
                                    lower=True, unit_diagonal=True)
    u = jnp.einsum('bhncm,bhnme->bhnce', T, v * beta[..., None])
    w = jnp.einsum('bhncm,bhnmd->bhncd', T, kb * jnp.exp(g)[..., None])
    qk = jnp.einsum('bhncd,bhnmd->bhncm', q, k) * decay
    q_dec = q * jnp.exp(g)[..., None]
    k_dec = k * jnp.exp(g[..., -1:] - g)[..., None]
    g_last = jnp.exp(g[..., -1])

    def step(state, inp):
        qd_i, kd_i, u_i, w_i, qk_i, gl_i = inp
        v_new = u_i - jnp.einsum('bhcd,bhde->bhce', w_i, state)
        o_i = jnp.einsum('bhcd,bhde->bhce', qd_i, state) + jnp.einsum('bhcm,bhme->bhce', qk_i, v_new)
        state = state * gl_i[..., None, None] + jnp.einsum('bhcd,bhce->bhde', kd_i, v_new)
        return state, o_i

    xs = (jnp.moveaxis(q_dec, 2, 0), jnp.moveaxis(k_dec, 2, 0), jnp.moveaxis(u, 2, 0),
          jnp.moveaxis(w, 2, 0), jnp.moveaxis(qk, 2, 0), jnp.moveaxis(g_last, 2, 0))
    state0 = jnp.zeros((B, H, Dk, Dv), jnp.float32)
    _, o = lax.scan(step, state0, xs)
    return jnp.moveaxis(o, 0, 2).reshape(B, H, S, Dv).astype(out_dtype)


def gated_deltanet(q, k, v, z, b, a, conv_w, A_log, dt_bias, out_gain):
    B, S, _ = q.shape
    qkv = jnp.concatenate([q, k, v], axis=-1)
    qkv = lax.conv_general_dilated(qkv, conv_w[:, None, :].astype(qkv.dtype), window_strides=(1,),
                                   padding=[(CONV_K // 2, CONV_K // 2)],
                                   dimension_numbers=('NWC', 'WIO', 'NWC'),
                                   feature_group_count=qkv.shape[-1])
    q, k, v = jnp.split(jax.nn.silu(qkv), 3, axis=-1)

    def heads(t):
        return t.reshape(B, S, DN_HEADS, DN_HEAD_DIM).transpose(0, 2, 1, 3)

    q = l2_norm(heads(q)) * (DN_HEAD_DIM ** -0.5)
    k = l2_norm(heads(k))
    v = heads(v)
    beta = jax.nn.sigmoid(b.astype(jnp.float32)).reshape(B, S, 2, DN_HEADS).transpose(2, 0, 3, 1)
    a_in = a.astype(jnp.float32).reshape(B, S, 2, DN_HEADS).transpose(2, 0, 3, 1)
    g = -jnp.exp(A_log.astype(jnp.float32))[:, None, :, None] * jax.nn.softplus(
        a_in + dt_bias.astype(jnp.float32)[:, None, :, None])
    o_fwd = gated_delta_rule_chunked(q, k, v, g[0], beta[0])
    o_bwd = jnp.flip(gated_delta_rule_chunked(jnp.flip(q, 2), jnp.flip(k, 2), jnp.flip(v, 2),
                                              jnp.flip(g[1], 2), jnp.flip(beta[1], 2)), 2)
    o = (o_fwd + o_bwd).transpose(0, 2, 1, 3)
    o = rms_norm(o, out_gain) * jax.nn.silu(z.reshape(B, S, DN_HEADS, DN_HEAD_DIM))
    return o.reshape(B, S, DN_WIDTH)


def setup_inputs(seed: int = 0) -> dict:
    key = jax.random.key(seed)
    ks = jax.random.split(key, 20)
    f32 = jnp.float32

    def nrm(k, shape, scale):
        return jax.random.normal(k, shape, f32) * scale

    x = nrm(ks[0], (BATCH, SEQ, D_MODEL), 1.0)
    c = nrm(ks[1], (BATCH, D_MODEL), 1.0)
    ada_w = nrm(ks[2], (DEPTH, D_MODEL, 6 * D_MODEL), D_MODEL ** -0.5)
    ada_b = nrm(ks[3], (DEPTH, 6 * D_MODEL), 0.02)
    norm1_g = 1.0 + nrm(ks[4], (DEPTH, D_MODEL), 0.05)
    norm2_g = 1.0 + nrm(ks[5], (DEPTH, D_MODEL), 0.05)
    w_in = nrm(ks[6], (DEPTH, D_MODEL, IN_COLS), D_MODEL ** -0.5)
    dn_conv = nrm(ks[7], (DEPTH, CONV_K, 3 * DN_WIDTH), CONV_K ** -0.5)
    dn_A_log = jnp.log(jax.random.uniform(ks[8], (DEPTH, 2, DN_HEADS), f32, 1.0, 16.0))
    dt = jnp.exp(jax.random.uniform(ks[9], (DEPTH, 2, DN_HEADS), f32, math.log(1e-3), math.log(1e-1)))
    dn_dt_bias = dt + jnp.log(-jnp.expm1(-dt))
    dn_out_gain = 1.0 + nrm(ks[10], (DEPTH, DN_HEAD_DIM), 0.05)
    qk_norm_gain = 1.0 + nrm(ks[11], (DEPTH, 2, DA_HEAD_DIM), 0.05)
    diff_lambda = nrm(ks[12], (DEPTH, 4, DA_HEAD_DIM), 0.1)
    diff_subln_gain = 1.0 + nrm(ks[13], (DEPTH, 2 * DA_HEAD_DIM), 0.05)
    w_branch_a = nrm(ks[14], (DEPTH, DA_WIDTH, D_MODEL), DA_WIDTH ** -0.5)
    w_branch_b = nrm(ks[15], (DEPTH, DN_WIDTH, D_MODEL), DN_WIDTH ** -0.5)
    w_out = nrm(ks[16], (DEPTH, D_MODEL, D_MODEL), D_MODEL ** -0.5)
    w_mlp1 = nrm(ks[17], (DEPTH, D_MODEL, D_FF), D_MODEL ** -0.5)
    w_mlp2 = nrm(ks[18], (DEPTH, D_FF, D_MODEL), D_FF ** -0.5)
    return {'x': x, 'c': c, 'ada_w': ada_w, 'ada_b': ada_b, 'norm1_g': norm1_g, 'norm2_g': norm2_g,
            'w_in': w_in, 'dn_conv': dn_conv, 'dn_A_log': dn_A_log, 'dn_dt_bias': dn_dt_bias,
            'dn_out_gain': dn_out_gain, 'qk_norm_gain': qk_norm_gain, 'diff_lambda': diff_lambda,
            'diff_subln_gain': diff_subln_gain, 'w_branch_a': w_branch_a, 'w_branch_b': w_branch_b,
            'w_out': w_out, 'w_mlp1': w_mlp1, 'w_mlp2': w_mlp2}


def reference(x, c, ada_w, ada_b, norm1_g, norm2_g, w_in, dn_conv, dn_A_log, dn_dt_bias, dn_out_gain,
              qk_norm_gain, diff_lambda, diff_subln_gain, w_branch_a, w_branch_b, w_out, w_mlp1, w_mlp2):
    B, S, D = x.shape
    cos, sin = rope_tables(S, DA_HEAD_DIM, x.dtype)
    split_points = [int(p) for p in np.cumsum(IN_SPLITS)[:-1]]
    c_act = jax.nn.silu(c)
    for l in range(DEPTH):
        mod = c_act @ ada_w[l] + ada_b[l]
        sh1, sc1, gt1, sh2, sc2, gt2 = [m[:, None, :] for m in jnp.split(mod, 6, axis=-1)]
        h = rms_norm(x, norm1_g[l]) * (1.0 + sc1) + sh1
        proj = h @ w_in[l]
        da_q, da_k, da_v, dn_q, dn_k, dn_v, dn_z, dn_b, dn_a, gates = jnp.split(proj, split_points, axis=-1)
        lambda_init = 0.8 - 0.6 * math.exp(-0.3 * l)
        lq = diff_lambda[l].astype(jnp.float32)
        lam = jnp.exp(jnp.sum(lq[0] * lq[1])) - jnp.exp(jnp.sum(lq[2] * lq[3])) + lambda_init
        y_a = diff_attention(da_q.reshape(B, S, DA_HEADS, 2, DA_HEAD_DIM),
                             da_k.reshape(B, S, DA_HEADS, 2, DA_HEAD_DIM),
                             da_v.reshape(B, S, DA_HEADS, 2 * DA_HEAD_DIM),
                             lam, lambda_init, cos, sin, qk_norm_gain[l, 0], qk_norm_gain[l, 1],
                             diff_subln_gain[l])
        y_b = gated_deltanet(dn_q, dn_k, dn_v, dn_z, dn_b, dn_a, dn_conv[l], dn_A_log[l], dn_dt_bias[l],
                             dn_out_gain[l])
        g_a, g_b = jnp.split(jax.nn.sigmoid(gates), 2, axis=-1)
        merged = g_a * (y_a @ w_branch_a[l]) + g_b * (y_b @ w_branch_b[l])
        x = x + gt1 * (merged @ w_out[l])
        h = rms_norm(x, norm2_g[l]) * (1.0 + sc2) + sh2
        x = x + gt2 * (jnp.square(jax.nn.relu(h @ w_mlp1[l])) @ w_mlp2[l])
    return x
```

```python
import functools
import math

import jax
import jax.numpy as jnp
from jax import lax
from jax.experimental import pallas as pl
from jax.experimental.pallas import tpu as pltpu

DA_HEADS = 4
DA_HEAD_DIM = 64
DA_WIDTH = DA_HEADS * 2 * DA_HEAD_DIM
DN_HEADS = 4
DN_HEAD_DIM = 128
DN_WIDTH = DN_HEADS * DN_HEAD_DIM
CONV_K = 5
CHUNK = 64
ROPE_THETA = 10000.0
EPS = 1e-6

LANES = 128
SUBLANES = 8
VMEM_LIMIT_BYTES = 56 * 1024 * 1024

D_MODEL = 1024
COL_GATE = 0
COL_DA_Q = 2 * D_MODEL // LANES
COL_DA_K = COL_DA_Q + DA_WIDTH // LANES
COL_DA_V = COL_DA_K + DA_WIDTH // LANES
COL_DN_Q = COL_DA_V + DA_WIDTH // LANES
COL_DN_K = COL_DN_Q + DN_WIDTH // LANES
COL_DN_V = COL_DN_K + DN_WIDTH // LANES
COL_DN_Z = COL_DN_V + DN_WIDTH // LANES


def _params(*sem):
    return pltpu.CompilerParams(dimension_semantics=sem, vmem_limit_bytes=VMEM_LIMIT_BYTES)


def _silu(x):
    return x * jax.nn.sigmoid(x)


def _softplus(x):
    return jnp.maximum(x, 0.0) + jnp.log(1.0 + jnp.exp(-jnp.abs(x)))


def _dot(a, b):
    return jnp.dot(a, b, preferred_element_type=jnp.float32)


def _dot_nt(a, b):
    return lax.dot_general(a, b, (((1,), (1,)), ((), ())), preferred_element_type=jnp.float32)


def _split3(x):
    hi = x.astype(jnp.bfloat16).astype(jnp.float32)
    r1 = x - hi
    mid = r1.astype(jnp.bfloat16).astype(jnp.float32)
    lo = (r1 - mid).astype(jnp.bfloat16).astype(jnp.float32)
    return hi, mid, lo


def _adaln_kernel(c_ref, w_ref, b_ref, o_ref):
    c_act = _silu(c_ref[...])
    o_ref[0] = jnp.dot(c_act, w_ref[0], preferred_element_type=jnp.float32,
                       precision=lax.Precision.HIGHEST) + b_ref[0]


def _adaln(c_pad, ada_w, ada_b):
    depth, d, n = ada_w.shape
    tn = 1536
    return pl.pallas_call(
        _adaln_kernel,
        grid=(depth, n // tn),
        in_specs=[
            pl.BlockSpec((SUBLANES, d), lambda l, j: (0, 0)),
            pl.BlockSpec((1, d, tn), lambda l, j: (l, 0, j)),
            pl.BlockSpec((1, 1, tn), lambda l, j: (l, 0, j)),
        ],
        out_specs=pl.BlockSpec((1, SUBLANES, tn), lambda l, j: (l, 0, j)),
        out_shape=jax.ShapeDtypeStruct((depth, SUBLANES, n), jnp.float32),
        compiler_params=_params("parallel", "parallel"),
        name="adaln",
    )(c_pad, ada_w, ada_b.reshape(depth, 1, n))


def _modulated_norm(x, gain, scale, shift):
    y = x * lax.rsqrt(jnp.mean(x * x, axis=-1, keepdims=True) + EPS)
    return (y * gain) * (1.0 + scale) + shift


def _inproj_kernel(x_ref, g_ref, sc_ref, sh_ref, w_ref, o_ref, h_ref):
    @pl.when(pl.program_id(2) == 0)
    def _():
        h_ref[...] = _modulated_norm(x_ref[0], g_ref[...], sc_ref[0], sh_ref[0]).astype(h_ref.dtype)

    o_ref[0] = _dot(h_ref[...], w_ref[...]).astype(o_ref.dtype)


def _inproj(x, gain, scale, shift, w, tm, tn):
    b, s, d = x.shape
    n = w.shape[1]
    return pl.pallas_call(
        _inproj_kernel,
        grid=(b, s // tm, n // tn),
        in_specs=[
            pl.BlockSpec((1, tm, d), lambda bi, i, j: (bi, i, 0)),
            pl.BlockSpec((1, d), lambda bi, i, j: (0, 0)),
            pl.BlockSpec((1, 1, d), lambda bi, i, j: (bi, 0, 0)),
            pl.BlockSpec((1, 1, d), lambda bi, i, j: (bi, 0, 0)),
            pl.BlockSpec((d, tn), lambda bi, i, j: (0, j)),
        ],
        out_specs=pl.BlockSpec((1, tm, tn), lambda bi, i, j: (bi, i, j)),
        out_shape=jax.ShapeDtypeStruct((b, s, n), jnp.float32),
        scratch_shapes=[pltpu.VMEM((tm, d), jnp.bfloat16)],
        compiler_params=_params("parallel", "parallel", "arbitrary"),
        name="inproj",
    )(x, gain, scale, shift, w)


def _attn_prep_kernel(q_ref, k_ref, v_ref, cos_ref, sin_ref, qg_ref, kg_ref, qo_ref, kt_ref, vo_ref):
    lane = lax.broadcasted_iota(jnp.int32, (1, LANES), 1)
    left = lane < DA_HEAD_DIM
    first_half = (lane % DA_HEAD_DIM) < (DA_HEAD_DIM // 2)
    cos = cos_ref[...]
    sin = sin_ref[...]

    def norm_rope(x, gain):
        sq = x * x
        s_all = jnp.sum(sq, axis=-1, keepdims=True)
        s_left = jnp.sum(jnp.where(left, sq, 0.0), axis=-1, keepdims=True)
        ms = jnp.where(left, s_left, s_all - s_left) * (1.0 / DA_HEAD_DIM)
        y = x * lax.rsqrt(ms + EPS) * gain
        fwd = pltpu.roll(y, LANES - DA_HEAD_DIM // 2, axis=1)
        bwd = pltpu.roll(y, DA_HEAD_DIM // 2, axis=1)
        return y * cos + jnp.where(first_half, fwd, bwd) * sin

    q = norm_rope(q_ref[0], qg_ref[...]) * (DA_HEAD_DIM ** -0.5)
    k = norm_rope(k_ref[0], kg_ref[...])
    qo_ref[0, 0, 0] = jnp.where(left, q, 0.0).astype(qo_ref.dtype)
    qo_ref[0, 0, 1] = jnp.where(left, 0.0, q).astype(qo_ref.dtype)
    kt_ref[0, 0] = k.T.astype(kt_ref.dtype)
    vo_ref[0, 0] = v_ref[0].astype(vo_ref.dtype)


def _attn_prep(proj, cos, sin_signed, qgain, kgain, ts):
    b, s, _ = proj.shape
    h = DA_HEADS
    col = lambda base: (lambda bi, hi, i: (bi, i, base + hi))
    return pl.pallas_call(
        _attn_prep_kernel,
        grid=(b, h, s // ts),
        in_specs=[
            pl.BlockSpec((1, ts, LANES), col(COL_DA_Q)),
            pl.BlockSpec((1, ts, LANES), col(COL_DA_K)),
            pl.BlockSpec((1, ts, LANES), col(COL_DA_V)),
            pl.BlockSpec((ts, LANES), lambda bi, hi, i: (i, 0)),
            pl.BlockSpec((ts, LANES), lambda bi, hi, i: (i, 0)),
            pl.BlockSpec((1, LANES), lambda bi, hi, i: (0, 0)),
            pl.BlockSpec((1, LANES), lambda bi, hi, i: (0, 0)),
        ],
        out_specs=[
            pl.BlockSpec((1, 1, 2, ts, LANES), lambda bi, hi, i: (bi, hi, 0, i, 0)),
            pl.BlockSpec((1, 1, LANES, ts), lambda bi, hi, i: (bi, hi, 0, i)),
            pl.BlockSpec((1, 1, ts, LANES), lambda bi, hi, i: (bi, hi, i, 0)),
        ],
        out_shape=[
            jax.ShapeDtypeStruct((b, h, 2, s, LANES), jnp.bfloat16),
            jax.ShapeDtypeStruct((b, h, LANES, s), jnp.bfloat16),
            jax.ShapeDtypeStruct((b, h, s, LANES), jnp.bfloat16),
        ],
        compiler_params=_params("parallel", "parallel", "parallel"),
        name="attn_prep",
    )(proj, proj, proj, cos, sin_signed, qgain, kgain)


def _attn_kernel(q_ref, kt_ref, v_ref, lam_ref, g_ref, o_ref, *, tq, tk, lambda_init):
    s_len = v_ref.shape[2]
    q = q_ref[0, 0].reshape(2 * tq, LANES)

    def body(j, carry):
        m, l, acc = carry
        off = pl.multiple_of(j * tk, tk)
        s = _dot(q, kt_ref[0, 0, :, pl.ds(off, tk)])
        m_new = jnp.maximum(m, jnp.max(s, axis=-1, keepdims=True))
        alpha = jnp.exp(m - m_new)
        p = jnp.exp(s - m_new)
        l = alpha * l + jnp.sum(p, axis=-1, keepdims=True)
        acc = alpha * acc + _dot(p.astype(jnp.bfloat16), v_ref[0, 0, pl.ds(off, tk), :])
        return m_new, l, acc

    init = (jnp.full((2 * tq, 1), -jnp.inf, jnp.float32), jnp.zeros((2 * tq, 1), jnp.float32),
            jnp.zeros((2 * tq, LANES), jnp.float32))
    _, l, acc = lax.fori_loop(0, s_len // tk, body, init)
    o = acc / l
    lq = lam_ref[...]
    lam = (jnp.exp(jnp.sum(lq[0:1] * lq[1:2], axis=-1, keepdims=True))
           - jnp.exp(jnp.sum(lq[2:3] * lq[3:4], axis=-1, keepdims=True)) + lambda_init)
    d = o[:tq] - lam * o[tq:]
    y = d * lax.rsqrt(jnp.mean(d * d, axis=-1, keepdims=True) + EPS) * g_ref[...]
    o_ref[0] = (y * (1.0 - lambda_init)).astype(o_ref.dtype)


def _attn(qs, kt, v, lam_params, subln_gain, lambda_init, tq, tk):
    b, h, _, s, _ = qs.shape
    return pl.pallas_call(
        functools.partial(_attn_kernel, tq=tq, tk=tk, lambda_init=lambda_init),
        grid=(b, h, s // tq),
        in_specs=[
            pl.BlockSpec((1, 1, 2, tq, LANES), lambda bi, hi, i: (bi, hi, 0, i, 0)),
            pl.BlockSpec((1, 1, LANES, s), lambda bi, hi, i: (bi, hi, 0, 0)),
            pl.BlockSpec((1, 1, s, LANES), lambda bi, hi, i: (bi, hi, 0, 0)),
            pl.BlockSpec(lam_params.shape, lambda bi, hi, i: (0, 0)),
            pl.BlockSpec((1, LANES), lambda bi, hi, i: (0, 0)),
        ],
        out_specs=pl.BlockSpec((1, tq, LANES), lambda bi, hi, i: (bi, i, hi)),
        out_shape=jax.ShapeDtypeStruct((b, s, h * LANES), jnp.float32),
        compiler_params=_params("parallel", "parallel", "parallel"),
        name="attn",
    )(qs, kt, v, lam_params, subln_gain)


INV_BLOCK = 16


def _nilpotent_inverse(m, eye, index):
    p = eye - m
    for _ in range(int(math.log2(index)) - 1):
        mb = m.astype(jnp.bfloat16)
        m = _dot(mb, mb)
        p = p + _dot(p.astype(jnp.bfloat16), m.astype(jnp.bfloat16))
    return p


def _gdn_prep_kernel(q_ref, qp_ref, qn_ref, k_ref, kp_ref, kn_ref, v_ref, vp_ref, vn_ref,
                     cw_ref, ba_ref, alog_ref, dtb_ref,
                     wq_ref, u_ref, qk_ref, kdt_ref, gl_ref,
                     qs_ref, ks_ref, vs_ref, *, t, nblk):
    i = pl.program_id(1)
    h = pl.program_id(2)
    pad = CONV_K // 2

    def conv_silu(x_ref, p_ref, n_ref, w):
        prev = jnp.where(i == 0, 0.0, p_ref[0])
        nxt = jnp.where(i == nblk - 1, 0.0, n_ref[0])
        xp = jnp.concatenate([prev, x_ref[0], nxt], axis=0)
        acc = jnp.zeros((t, LANES), jnp.float32)
        for j in range(CONV_K):
            lo = SUBLANES - pad + j
            acc = acc + xp[lo:lo + t] * w[j:j + 1]
        return _silu(acc)

    def l2n(x):
        return x * lax.rsqrt(jnp.sum(x * x, axis=-1, keepdims=True) + EPS)

    cw = cw_ref[...]
    qs_ref[...] = l2n(conv_silu(q_ref, qp_ref, qn_ref, cw[0])) * (DN_HEAD_DIM ** -0.5)
    ks_ref[...] = l2n(conv_silu(k_ref, kp_ref, kn_ref, cw[1]))
    vs_ref[...] = conv_silu(v_ref, vp_ref, vn_ref, cw[2])

    lane = lax.broadcasted_iota(jnp.int32, (1, LANES), 1)
    row = lax.broadcasted_iota(jnp.int32, (CHUNK, CHUNK), 0)
    colm = lax.broadcasted_iota(jnp.int32, (CHUNK, CHUNK), 1)
    eye = (row == colm).astype(jnp.float32)
    same_blk = (row // INV_BLOCK) == (colm // INV_BLOCK)
    neg_a = -jnp.exp(alog_ref[...])
    dtb = dtb_ref[...]

    def chunk_body(c, carry):
        r0 = pl.multiple_of(c * CHUNK, CHUNK)
        qc = qs_ref[pl.ds(r0, CHUNK), :]
        kc = ks_ref[pl.ds(r0, CHUNK), :]
        vc = vs_ref[pl.ds(r0, CHUNK), :]
        ba = ba_ref[0, pl.ds(r0, CHUNK), :]
        kcb = kc.astype(jnp.bfloat16)
        gram = _dot_nt(kcb, kcb)
        qk = _dot_nt(qc.astype(jnp.bfloat16), kcb)
        g_all = neg_a * _softplus(ba + dtb)
        beta_all = jax.nn.sigmoid(ba)
        for d in range(2):
            keep = (row >= colm) if d == 0 else (row <= colm)
            strict = (row > colm) if d == 0 else (row < colm)
            sel_b = lane == (d * DN_HEADS + h)
            sel_g = lane == (2 * DN_HEADS + d * DN_HEADS + h)
            beta = jnp.sum(jnp.where(sel_b, beta_all, 0.0), axis=-1, keepdims=True)
            g = jnp.sum(jnp.where(sel_g, g_all, 0.0), axis=-1, keepdims=True)
            ghi, gmid, glo = _split3(jnp.broadcast_to(g, (CHUNK, LANES)))
            tri = keep.astype(jnp.bfloat16)
            bf = jnp.bfloat16
            gc = (_dot(tri, ghi.astype(bf)) + _dot(tri, gmid.astype(bf)) + _dot(tri, glo.astype(bf)))[:, 0:1]
            gtot = jnp.sum(g, axis=0, keepdims=True)
            chi, cmid, clo = _split3(jnp.broadcast_to(gc, (CHUNK, LANES)))
            lhs = jnp.where(lane == 0, chi, jnp.where(lane == 1, cmid, jnp.where(lane == 2, clo,
                  jnp.where(lane < 6, 1.0, 0.0))))
            rhs = jnp.where(lane < 3, 1.0, jnp.where(lane == 3, -chi, jnp.where(lane == 4, -cmid,
                  jnp.where(lane == 5, -clo, 0.0))))
            e = _dot_nt(lhs.astype(bf), rhs.astype(bf))
            decay = jnp.where(keep, jnp.exp(jnp.where(keep, e, 0.0)), 0.0)
            lmat = jnp.where(strict, beta * gram * decay, 0.0)
            l_diag = jnp.where(same_blk, lmat, 0.0)
            d_inv = _nilpotent_inverse(l_diag, eye, INV_BLOCK)
            n_off = _dot(d_inv.astype(jnp.bfloat16), (lmat - l_diag).astype(jnp.bfloat16))
            n_inv = _nilpotent_inverse(n_off, eye, CHUNK // INV_BLOCK)
            t_inv = _dot(n_inv.astype(jnp.bfloat16), d_inv.astype(jnp.bfloat16))
            egc = jnp.exp(gc)
            rhs_uw = jnp.concatenate([vc * beta, kc * (beta * egc)], axis=1).astype(jnp.bfloat16)
            uw = _dot(t_inv.astype(jnp.bfloat16), rhs_uw)
            u_ref[d, 0, 0, pl.ds(r0, CHUNK), :] = uw[:, :LANES]
            r2 = pl.multiple_of(c * 2 * CHUNK, 2 * CHUNK)
            wq_ref[d, 0, 0, pl.ds(r2, CHUNK), :] = uw[:, LANES:].astype(wq_ref.dtype)
            wq_ref[d, 0, 0, pl.ds(r2 + CHUNK, CHUNK), :] = (qc * egc).astype(wq_ref.dtype)
            qk_ref[d, 0, 0, pl.ds(r0, CHUNK), :] = (qk * decay).astype(qk_ref.dtype)
            kd = kc * jnp.exp(gtot - gc)
            kdt_ref[d, 0, 0, pl.ds(r2, 2 * CHUNK), :] = kd.T.astype(kdt_ref.dtype)
            r8 = pl.multiple_of(c * SUBLANES, SUBLANES)
            gl_ref[d, 0, 0, pl.ds(r8, SUBLANES), :] = jnp.broadcast_to(jnp.exp(gtot), (SUBLANES, LANES))
        return carry

    lax.fori_loop(0, t // CHUNK, chunk_body, 0)


def _gdn_prep(proj, conv_w, a_log_row, dt_bias_row, t):
    b, s, ncol = proj.shape
    nh = DN_HEADS
    nblk = s // t
    n = s // CHUNK
    last_col = ncol // LANES - 1
    tpb = t // SUBLANES

    def main(base):
        return pl.BlockSpec((1, t, LANES), lambda bi, i, hi: (bi, i, base + hi))

    def prev(base):
        return pl.BlockSpec((1, SUBLANES, LANES),
                            lambda bi, i, hi: (bi, jnp.maximum(i * tpb - 1, 0), base + hi))

    def nxt(base):
        return pl.BlockSpec((1, SUBLANES, LANES),
                            lambda bi, i, hi: (bi, jnp.minimum((i + 1) * tpb, s // SUBLANES - 1), base + hi))

    in_specs = []
    for base in (COL_DN_Q, COL_DN_K, COL_DN_V):
        in_specs += [main(base), prev(base), nxt(base)]
    in_specs += [
        pl.BlockSpec((3, CONV_K, LANES), lambda bi, i, hi: (0, 0, hi)),
        pl.BlockSpec((1, t, LANES), lambda bi, i, hi: (bi, i, last_col)),
        pl.BlockSpec((1, LANES), lambda bi, i, hi: (0, 0)),
        pl.BlockSpec((1, LANES), lambda bi, i, hi: (0, 0)),
    ]
    out_specs = [
        pl.BlockSpec((2, 1, 1, 2 * t, LANES), lambda bi, i, hi: (0, bi, hi, i, 0)),
        pl.BlockSpec((2, 1, 1, t, LANES), lambda bi, i, hi: (0, bi, hi, i, 0)),
        pl.BlockSpec((2, 1, 1, t, CHUNK), lambda bi, i, hi: (0, bi, hi, i, 0)),
        pl.BlockSpec((2, 1, 1, 2 * t, CHUNK), lambda bi, i, hi: (0, bi, hi, i, 0)),
        pl.BlockSpec((2, 1, 1, t // CHUNK * SUBLANES, LANES), lambda bi, i, hi: (0, bi, hi, i, 0)),
    ]
    out_shape = [
        jax.ShapeDtypeStruct((2, b, nh, 2 * s, LANES), jnp.bfloat16),
        jax.ShapeDtypeStruct((2, b, nh, s, LANES), jnp.float32),
        jax.ShapeDtypeStruct((2, b, nh, s, CHUNK), jnp.bfloat16),
        jax.ShapeDtypeStruct((2, b, nh, 2 * s, CHUNK), jnp.bfloat16),
        jax.ShapeDtypeStruct((2, b, nh, n * SUBLANES, LANES), jnp.float32),
    ]
    proj9 = [proj] * 9
    return pl.pallas_call(
        functools.partial(_gdn_prep_kernel, t=t, nblk=nblk),
        grid=(b, nblk, nh),
        in_specs=in_specs,
        out_specs=out_specs,
        out_shape=out_shape,
        scratch_shapes=[pltpu.VMEM((t, LANES), jnp.float32)] * 3,
        compiler_params=_params("parallel", "parallel", "parallel"),
        name="gdn_prep",
    )(*proj9, conv_w, proj, a_log_row, dt_bias_row)


def _gdn_scan_kernel(wq_f, u_f, qk_f, kdt_f, gl_f, wq_b, u_b, qk_b, kdt_b, gl_b,
                     o_f, o_b, state_ref, *, t):
    nc = t // CHUNK

    @pl.when(pl.program_id(1) == 0)
    def _():
        state_ref[...] = jnp.zeros_like(state_ref)

    streams = ((0, wq_f, u_f, qk_f, kdt_f, gl_f, o_f), (1, wq_b, u_b, qk_b, kdt_b, gl_b, o_b))

    def body(c, carry):
        for d, wq, u, qk, kdt, gl, o in streams:
            cc = c if d == 0 else nc - 1 - c
            r0 = pl.multiple_of(cc * CHUNK, CHUNK)
            r2 = pl.multiple_of(cc * 2 * CHUNK, 2 * CHUNK)
            r8 = pl.multiple_of(cc * SUBLANES, SUBLANES)
            for hh in range(DN_HEADS):
                st = state_ref[d, hh]
                ws = _dot(wq[0, 0, hh, pl.ds(r2, 2 * CHUNK), :], st.astype(jnp.bfloat16))
                v_new = u[0, 0, hh, pl.ds(r0, CHUNK), :] - ws[:CHUNK]
                v_new_b = v_new.astype(jnp.bfloat16)
                o[0, 0, hh, pl.ds(r0, CHUNK), :] = ws[CHUNK:] + _dot(qk[0, 0, hh, pl.ds(r0, CHUNK), :], v_new_b)
                decay = gl[0, 0, hh, pl.ds(r8, SUBLANES), :][0:1]
                state_ref[d, hh] = st * decay + _dot(kdt[0, 0, hh, pl.ds(r2, 2 * CHUNK), :], v_new_b)
        return carry

    lax.fori_loop(0, nc, body, 0)


def _gdn_scan(wq, u, qk, kdt, gl, t):
    _, b, nh, s, _ = u.shape
    nblk = s // t
    g8 = t // CHUNK * SUBLANES

    def specs(d):
        if d == 0:
            idx = lambda bi, i: (0, bi, 0, i, 0)
        else:
            idx = lambda bi, i: (1, bi, 0, nblk - 1 - i, 0)
        return [
            pl.BlockSpec((1, 1, nh, 2 * t, LANES), idx),
            pl.BlockSpec((1, 1, nh, t, LANES), idx),
            pl.BlockSpec((1, 1, nh, t, CHUNK), idx),
            pl.BlockSpec((1, 1, nh, 2 * t, CHUNK), idx),
            pl.BlockSpec((1, 1, nh, g8, LANES), idx),
        ], pl.BlockSpec((1, 1, nh, t, LANES), idx)

    in_f, out_f = specs(0)
    in_b, out_b = specs(1)
    o_shape = jax.ShapeDtypeStruct((2, b, nh, s, LANES), jnp.float32)
    return pl.pallas_call(
        functools.partial(_gdn_scan_kernel, t=t),
        grid=(b, nblk),
        in_specs=in_f + in_b,
        out_specs=[out_f, out_b],
        out_shape=[o_shape, o_shape],
        scratch_shapes=[pltpu.VMEM((2, nh, DN_HEAD_DIM, DN_HEAD_DIM), jnp.float32)],
        compiler_params=_params("parallel", "arbitrary"),
        name="gdn_scan",
    )(wq, u, qk, kdt, gl, wq, u, qk, kdt, gl)


def _merge_kernel(of_ref, ob_ref, z_ref, ga_ref, gb_ref, ya_ref, x_ref, gt_ref, og_ref,
                  wa_ref, wb_ref, wo_ref, o_ref):
    og = og_ref[...]
    z = z_ref[0]
    parts = []
    for hh in range(DN_HEADS):
        o = of_ref[0, 0, hh] + ob_ref[0, 0, hh]
        y = o * lax.rsqrt(jnp.mean(o * o, axis=-1, keepdims=True) + EPS) * og
        parts.append((y * _silu(z[:, hh * LANES:(hh + 1) * LANES])).astype(jnp.bfloat16))
    yb = jnp.concatenate(parts, axis=1)
    pa = _dot(ya_ref[0].astype(jnp.bfloat16), wa_ref[...])
    pb = _dot(yb, wb_ref[...])
    merged = jax.nn.sigmoid(ga_ref[0]) * pa + jax.nn.sigmoid(gb_ref[0]) * pb
    o_ref[0] = x_ref[0] + gt_ref[0] * _dot(merged.astype(jnp.bfloat16), wo_ref[...])


def _merge(o_f, o_b, proj, y_a, x, gate, out_gain, w_a, w_b, w_o, tm):
    b, s, d = x.shape
    nh = DN_HEADS
    zw = DN_WIDTH // LANES
    return pl.pallas_call(
        _merge_kernel,
        grid=(b, s // tm),
        in_specs=[
            pl.BlockSpec((1, 1, nh, tm, LANES), lambda bi, i: (0, bi, 0, i, 0)),
            pl.BlockSpec((1, 1, nh, tm, LANES), lambda bi, i: (1, bi, 0, i, 0)),
            pl.BlockSpec((1, tm, DN_WIDTH), lambda bi, i: (bi, i, COL_DN_Z // zw)),
            pl.BlockSpec((1, tm, d), lambda bi, i: (bi, i, 0)),
            pl.BlockSpec((1, tm, d), lambda bi, i: (bi, i, 1)),
            pl.BlockSpec((1, tm, DA_WIDTH), lambda bi, i: (bi, i, 0)),
            pl.BlockSpec((1, tm, d), lambda bi, i: (bi, i, 0)),
            pl.BlockSpec((1, 1, d), lambda bi, i: (bi, 0, 0)),
            pl.BlockSpec((1, LANES), lambda bi, i: (0, 0)),
            pl.BlockSpec(w_a.shape, lambda bi, i: (0, 0)),
            pl.BlockSpec(w_b.shape, lambda bi, i: (0, 0)),
            pl.BlockSpec(w_o.shape, lambda bi, i: (0, 0)),
        ],
        out_specs=pl.BlockSpec((1, tm, d), lambda bi, i: (bi, i, 0)),
        out_shape=jax.ShapeDtypeStruct((b, s, d), jnp.float32),
        compiler_params=_params("parallel", "parallel"),
        name="merge",
    )(o_f, o_b, proj, proj, proj, y_a, x, gate, out_gain, w_a, w_b, w_o)


def _mlp_kernel(x_ref, g_ref, sc_ref, sh_ref, gt_ref, w1_ref, w2_ref, o_ref, h_ref, acc_ref):
    j = pl.program_id(2)

    @pl.when(j == 0)
    def _():
        h_ref[...] = _modulated_norm(x_ref[0], g_ref[...], sc_ref[0], sh_ref[0]).astype(h_ref.dtype)
        acc_ref[...] = jnp.zeros_like(acc_ref)

    a = jnp.maximum(_dot(h_ref[...], w1_ref[...]), 0.0)
    acc_ref[...] += _dot((a * a).astype(jnp.bfloat16), w2_ref[...])

    @pl.when(j == pl.num_programs(2) - 1)
    def _():
        o_ref[0] = x_ref[0] + gt_ref[0] * acc_ref[...]


def _mlp(x, gain, scale, shift, gate, w1, w2, tm, tf):
    b, s, d = x.shape
    f = w1.shape[1]
    vec = pl.BlockSpec((1, 1, d), lambda bi, i, j: (bi, 0, 0))
    return pl.pallas_call(
        _mlp_kernel,
        grid=(b, s // tm, f // tf),
        in_specs=[
            pl.BlockSpec((1, tm, d), lambda bi, i, j: (bi, i, 0)),
            pl.BlockSpec((1, d), lambda bi, i, j: (0, 0)),
            vec, vec, vec,
            pl.BlockSpec((d, tf), lambda bi, i, j: (0, j)),
            pl.BlockSpec((tf, d), lambda bi, i, j: (j, 0)),
        ],
        out_specs=pl.BlockSpec((1, tm, d), lambda bi, i, j: (bi, i, 0)),
        out_shape=jax.ShapeDtypeStruct((b, s, d), jnp.float32),
        scratch_shapes=[pltpu.VMEM((tm, d), jnp.bfloat16), pltpu.VMEM((tm, d), jnp.float32)],
        compiler_params=_params("parallel", "parallel", "arbitrary"),
        name="mlp",
    )(x, gain, scale, shift, gate, w1, w2)


def _reorder_w_in(w_in, d_model):
    assert d_model == D_MODEL
    main = 3 * DA_WIDTH + 4 * DN_WIDTH
    small = 4 * DN_HEADS
    head = w_in[:, :main]
    ba = w_in[:, main:main + small]
    gates = w_in[:, main + small:]
    pad = jnp.zeros((w_in.shape[0], LANES - small), w_in.dtype)
    return jnp.concatenate([gates, head, ba, pad], axis=1).astype(jnp.bfloat16)


def _rope_tables(seq):
    pos = jnp.arange(seq, dtype=jnp.float32)
    inv = 1.0 / (ROPE_THETA ** (jnp.arange(0, DA_HEAD_DIM, 2, dtype=jnp.float32) / DA_HEAD_DIM))
    ang = pos[:, None] * inv[None, :]
    ang = jnp.concatenate([ang, ang, ang, ang], axis=-1)
    sign = jnp.where((jnp.arange(LANES) % DA_HEAD_DIM) < DA_HEAD_DIM // 2, -1.0, 1.0)
    return jnp.cos(ang), jnp.sin(ang) * sign


def kernel(x, c, ada_w, ada_b, norm1_g, norm2_g, w_in, dn_conv, dn_A_log, dn_dt_bias, dn_out_gain,
           qk_norm_gain, diff_lambda, diff_subln_gain, w_branch_a, w_branch_b, w_out, w_mlp1, w_mlp2):
    b, s, d = x.shape
    depth = ada_w.shape[0]
    assert s % 512 == 0 and d % LANES == 0 and b <= SUBLANES
    tm = min(1024, s)
    t_gdn = min(512, s)
    tq = min(256, s)
    tk = min(512, s)

    c_pad = jnp.zeros((SUBLANES, d), jnp.float32).at[:b].set(c)
    mod = _adaln(c_pad, ada_w, ada_b)[:, :b]
    cos, sin_signed = _rope_tables(s)
    small = 4 * DN_HEADS

    for l in range(depth):
        sh1, sc1, gt1, sh2, sc2, gt2 = [mod[l, :, None, k * d:(k + 1) * d] for k in range(6)]
        lambda_init = 0.8 - 0.6 * math.exp(-0.3 * l)

        w_in_l = _reorder_w_in(w_in[l], d)
        proj = _inproj(x, norm1_g[l][None], sc1, sh1, w_in_l, tm, 1152)

        qgain = jnp.tile(qk_norm_gain[l, 0], 2)[None]
        kgain = jnp.tile(qk_norm_gain[l, 1], 2)[None]
        qs, kt, v = _attn_prep(proj, cos, sin_signed, qgain, kgain, min(512, s))
        y_a = _attn(qs, kt, v, diff_lambda[l], diff_subln_gain[l][None], lambda_init, tq, tk)

        conv_w = dn_conv[l].reshape(CONV_K, 3, DN_WIDTH).transpose(1, 0, 2)
        row = jnp.zeros((1, LANES), jnp.float32)
        a_log_row = row.at[0, small // 2:small].set(dn_A_log[l].reshape(-1))
        dt_bias_row = row.at[0, small // 2:small].set(dn_dt_bias[l].reshape(-1))
        wq, u, qk, kdt, gl = _gdn_prep(proj, conv_w, a_log_row, dt_bias_row, t_gdn)
        o_f, o_b = _gdn_scan(wq, u, qk, kdt, gl, t_gdn)

        x = _merge(o_f, o_b, proj, y_a, x, gt1, dn_out_gain[l][None],
                   w_branch_a[l].astype(jnp.bfloat16), w_branch_b[l].astype(jnp.bfloat16),
                   w_out[l].astype(jnp.bfloat16), min(512, s))
        x = _mlp(x, norm2_g[l][None], sc2, sh2, gt2,
                 w_mlp1[l].astype(jnp.bfloat16), w_mlp2[l].astype(jnp.bfloat16), tm, 1024)
    return x
```

```python
import functools
import math

import jax
import jax.numpy as jnp
from jax import lax
from jax.experimental import pallas as pl
from jax.experimental.pallas import tpu as pltpu

DA_HEADS = 4
DA_HEAD_DIM = 64
DA_WIDTH = DA_HEADS * 2 * DA_HEAD_DIM
DN_HEADS = 4
DN_HEAD_DIM = 128
DN_WIDTH = DN_HEADS * DN_HEAD_DIM
CONV_K = 5
CHUNK = 64
ROPE_THETA = 10000.0
EPS = 1e-6
LOG2_E = math.log2(math.e)

LANES = 128
SUBLANES = 8
VMEM_LIMIT_BYTES = 56 * 1024 * 1024

D_MODEL = 1024
COL_GATE = 0
COL_DA_Q = 2 * D_MODEL // LANES
COL_DA_K = COL_DA_Q + DA_WIDTH // LANES
COL_DA_V = COL_DA_K + DA_WIDTH // LANES
COL_DN_Q = COL_DA_V + DA_WIDTH // LANES
COL_DN_K = COL_DN_Q + DN_WIDTH // LANES
COL_DN_V = COL_DN_K + DN_WIDTH // LANES
COL_DN_Z = COL_DN_V + DN_WIDTH // LANES


def _params(*sem):
    return pltpu.CompilerParams(dimension_semantics=sem, vmem_limit_bytes=VMEM_LIMIT_BYTES)


def _silu(x):
    return x * jax.nn.sigmoid(x)


def _softplus(x):
    return jnp.maximum(x, 0.0) + jnp.log(1.0 + jnp.exp(-jnp.abs(x)))


def _dot(a, b):
    return jnp.dot(a, b, preferred_element_type=jnp.float32)


def _dot_nt(a, b):
    return lax.dot_general(a, b, (((1,), (1,)), ((), ())), preferred_element_type=jnp.float32)


def _split3(x):
    hi = x.astype(jnp.bfloat16).astype(jnp.float32)
    r1 = x - hi
    mid = r1.astype(jnp.bfloat16).astype(jnp.float32)
    lo = (r1 - mid).astype(jnp.bfloat16).astype(jnp.float32)
    return hi, mid, lo


def _adaln_kernel(c_ref, w_ref, b_ref, o_ref):
    c_act = _silu(c_ref[...])
    o_ref[0] = jnp.dot(c_act, w_ref[0], preferred_element_type=jnp.float32,
                       precision=lax.Precision.HIGHEST) + b_ref[0]


def _adaln(c_pad, ada_w, ada_b):
    depth, d, n = ada_w.shape
    tn = 1536
    return pl.pallas_call(
        _adaln_kernel,
        grid=(depth, n // tn),
        in_specs=[
            pl.BlockSpec((SUBLANES, d), lambda l, j: (0, 0)),
            pl.BlockSpec((1, d, tn), lambda l, j: (l, 0, j)),
            pl.BlockSpec((1, 1, tn), lambda l, j: (l, 0, j)),
        ],
        out_specs=pl.BlockSpec((1, SUBLANES, tn), lambda l, j: (l, 0, j)),
        out_shape=jax.ShapeDtypeStruct((depth, SUBLANES, n), jnp.float32),
        compiler_params=_params("parallel", "parallel"),
        name="adaln",
    )(c_pad, ada_w, ada_b.reshape(depth, 1, n))


def _modulated_norm(x, gain, scale, shift):
    y = x * lax.rsqrt(jnp.mean(x * x, axis=-1, keepdims=True) + EPS)
    return (y * gain) * (1.0 + scale) + shift


def _inproj_kernel(x_ref, g_ref, sc_ref, sh_ref, w_ref, o_ref, h_ref):
    @pl.when(pl.program_id(2) == 0)
    def _():
        h_ref[...] = _modulated_norm(x_ref[0], g_ref[...], sc_ref[0], sh_ref[0]).astype(h_ref.dtype)

    o_ref[0] = _dot(h_ref[...], w_ref[...]).astype(o_ref.dtype)


def _inproj(x, gain, scale, shift, w, tm, tn):
    b, s, d = x.shape
    n = w.shape[1]
    return pl.pallas_call(
        _inproj_kernel,
        grid=(b, s // tm, n // tn),
        in_specs=[
            pl.BlockSpec((1, tm, d), lambda bi, i, j: (bi, i, 0)),
            pl.BlockSpec((1, d), lambda bi, i, j: (0, 0)),
            pl.BlockSpec((1, 1, d), lambda bi, i, j: (bi, 0, 0)),
            pl.BlockSpec((1, 1, d), lambda bi, i, j: (bi, 0, 0)),
            pl.BlockSpec((d, tn), lambda bi, i, j: (0, j)),
        ],
        out_specs=pl.BlockSpec((1, tm, tn), lambda bi, i, j: (bi, i, j)),
        out_shape=jax.ShapeDtypeStruct((b, s, n), jnp.float32),
        scratch_shapes=[pltpu.VMEM((tm, d), jnp.bfloat16)],
        compiler_params=_params("parallel", "parallel", "arbitrary"),
        name="inproj",
    )(x, gain, scale, shift, w)


def _attn_prep_kernel(q_ref, k_ref, v_ref, cos_ref, sin_ref, qg_ref, kg_ref, qo_ref, kt_ref, vo_ref):
    lane = lax.broadcasted_iota(jnp.int32, (1, LANES), 1)
    left = lane < DA_HEAD_DIM
    first_half = (lane % DA_HEAD_DIM) < (DA_HEAD_DIM // 2)
    cos = cos_ref[...]
    sin = sin_ref[...]

    def norm_rope(x, gain):
        sq = x * x
        s_all = jnp.sum(sq, axis=-1, keepdims=True)
        s_left = jnp.sum(jnp.where(left, sq, 0.0), axis=-1, keepdims=True)
        ms = jnp.where(left, s_left, s_all - s_left) * (1.0 / DA_HEAD_DIM)
        y = x * lax.rsqrt(ms + EPS) * gain
        fwd = pltpu.roll(y, LANES - DA_HEAD_DIM // 2, axis=1)
        bwd = pltpu.roll(y, DA_HEAD_DIM // 2, axis=1)
        return y * cos + jnp.where(first_half, fwd, bwd) * sin

    q = norm_rope(q_ref[0], qg_ref[...]) * (DA_HEAD_DIM ** -0.5 * LOG2_E)
    k = norm_rope(k_ref[0], kg_ref[...])
    qo_ref[0, 0, 0] = jnp.where(left, q, 0.0).astype(qo_ref.dtype)
    qo_ref[0, 0, 1] = jnp.where(left, 0.0, q).astype(qo_ref.dtype)
    kt_ref[0, 0] = k.T.astype(kt_ref.dtype)
    v = v_ref[0]
    vo_ref[0, 0] = jnp.concatenate([v, jnp.ones_like(v)], axis=1).astype(vo_ref.dtype)


def _attn_prep(proj, cos, sin_signed, qgain, kgain, ts):
    b, s, _ = proj.shape
    h = DA_HEADS
    col = lambda base: (lambda bi, hi, i: (bi, i, base + hi))
    return pl.pallas_call(
        _attn_prep_kernel,
        grid=(b, h, s // ts),
        in_specs=[
            pl.BlockSpec((1, ts, LANES), col(COL_DA_Q)),
            pl.BlockSpec((1, ts, LANES), col(COL_DA_K)),
            pl.BlockSpec((1, ts, LANES), col(COL_DA_V)),
            pl.BlockSpec((ts, LANES), lambda bi, hi, i: (i, 0)),
            pl.BlockSpec((ts, LANES), lambda bi, hi, i: (i, 0)),
            pl.BlockSpec((1, LANES), lambda bi, hi, i: (0, 0)),
            pl.BlockSpec((1, LANES), lambda bi, hi, i: (0, 0)),
        ],
        out_specs=[
            pl.BlockSpec((1, 1, 2, ts, LANES), lambda bi, hi, i: (bi, hi, 0, i, 0)),
            pl.BlockSpec((1, 1, LANES, ts), lambda bi, hi, i: (bi, hi, 0, i)),
            pl.BlockSpec((1, 1, ts, 2 * LANES), lambda bi, hi, i: (bi, hi, i, 0)),
        ],
        out_shape=[
            jax.ShapeDtypeStruct((b, h, 2, s, LANES), jnp.bfloat16),
            jax.ShapeDtypeStruct((b, h, LANES, s), jnp.bfloat16),
            jax.ShapeDtypeStruct((b, h, s, 2 * LANES), jnp.bfloat16),
        ],
        compiler_params=_params("parallel", "parallel", "parallel"),
        name="attn_prep",
    )(proj, proj, proj, cos, sin_signed, qgain, kgain)


ATTN_UNROLL = 4


ATTN_ROW_BLOCK = 32


def _attn_kernel(q_ref, kt_ref, v_ref, lam_ref, g_ref, o_ref, s_ref, p_ref, m_ref, alpha_ref, acc_ref,
                 *, tq, tk, lambda_init):
    s_len = v_ref.shape[2]
    q = q_ref[0, 0].reshape(2 * tq, LANES)
    n_kv = s_len // tk
    groups = tk // LANES
    rb = ATTN_ROW_BLOCK

    m_ref[...] = jnp.full(m_ref.shape, -jnp.inf, jnp.float32)
    acc_ref[...] = jnp.zeros(acc_ref.shape, jnp.float32)

    def scores(j, slot):
        s_ref[slot] = _dot(q, kt_ref[0, 0, :, pl.ds(pl.multiple_of(j * tk, tk), tk)])

    def update(j, slot):
        for r in range(2 * tq // rb):
            rows = slice(r * rb, (r + 1) * rb)
            sg = [s_ref[slot, rows, g * LANES:(g + 1) * LANES] for g in range(groups)]
            m_old = m_ref[rows, :]
            m_new = jnp.maximum(m_old, jnp.max(functools.reduce(jnp.maximum, sg), axis=-1, keepdims=True))
            alpha_ref[rows, :] = jnp.exp2(m_old - m_new)
            m_ref[rows, :] = m_new
            for g in range(groups):
                p_ref[rows, g * LANES:(g + 1) * LANES] = jnp.exp2(sg[g] - m_new).astype(jnp.bfloat16)
        alpha = alpha_ref[...]
        pv = _dot(p_ref[...], v_ref[0, 0, pl.ds(pl.multiple_of(j * tk, tk), tk), :])
        acc_ref[...] = jnp.concatenate([alpha, alpha], axis=1) * acc_ref[...] + pv

    def body(jj, carry):
        j = ATTN_UNROLL * jj
        for k in range(ATTN_UNROLL):
            scores(j + k + 1, (k + 1) % 2)
            update(j + k, k % 2)
        return carry

    scores(0, 0)
    lax.fori_loop(0, n_kv // ATTN_UNROLL - 1, body, 0)
    for j in range(n_kv - ATTN_UNROLL, n_kv):
        if j + 1 < n_kv:
            scores(j + 1, (j + 1) % 2)
        update(j, j % 2)
    o = acc_ref[:, :LANES] / acc_ref[:, LANES:]
    lq = lam_ref[...]
    lam = (jnp.exp(jnp.sum(lq[0:1] * lq[1:2], axis=-1, keepdims=True))
           - jnp.exp(jnp.sum(lq[2:3] * lq[3:4], axis=-1, keepdims=True)) + lambda_init)
    d = o[:tq] - lam * o[tq:]
    y = d * lax.rsqrt(jnp.mean(d * d, axis=-1, keepdims=True) + EPS) * g_ref[...]
    o_ref[0] = (y * (1.0 - lambda_init)).astype(o_ref.dtype)


def _attn(qs, kt, v, lam_params, subln_gain, lambda_init, tq, tk):
    b, h, _, s, _ = qs.shape
    assert s % (tk * ATTN_UNROLL) == 0 and ATTN_UNROLL % 2 == 0 and (2 * tq) % ATTN_ROW_BLOCK == 0
    return pl.pallas_call(
        functools.partial(_attn_kernel, tq=tq, tk=tk, lambda_init=lambda_init),
        grid=(b, h, s // tq),
        in_specs=[
            pl.BlockSpec((1, 1, 2, tq, LANES), lambda bi, hi, i: (bi, hi, 0, i, 0)),
            pl.BlockSpec((1, 1, LANES, s), lambda bi, hi, i: (bi, hi, 0, 0)),
            pl.BlockSpec((1, 1, s, 2 * LANES), lambda bi, hi, i: (bi, hi, 0, 0)),
            pl.BlockSpec(lam_params.shape, lambda bi, hi, i: (0, 0)),
            pl.BlockSpec((1, LANES), lambda bi, hi, i: (0, 0)),
        ],
        out_specs=pl.BlockSpec((1, tq, LANES), lambda bi, hi, i: (bi, i, hi)),
        out_shape=jax.ShapeDtypeStruct((b, s, h * LANES), jnp.float32),
        scratch_shapes=[pltpu.VMEM((2, 2 * tq, tk), jnp.float32), pltpu.VMEM((2 * tq, tk), jnp.bfloat16),
                        pltpu.VMEM((2 * tq, LANES), jnp.float32), pltpu.VMEM((2 * tq, LANES), jnp.float32),
                        pltpu.VMEM((2 * tq, 2 * LANES), jnp.float32)],
        compiler_params=_params("parallel", "parallel", "parallel"),
        name="attn",
    )(qs, kt, v, lam_params, subln_gain)


INV_BLOCK = 16


GDN_UNROLL = 8


def _gdn_prep_kernel(q_ref, qp_ref, qn_ref, k_ref, kp_ref, kn_ref, v_ref, vp_ref, vn_ref,
                     cw_ref, ba_ref, alog_ref, dtb_ref,
                     wq_ref, u_ref, qkkd_ref, gl_ref,
                     qs_ref, ks_ref, vs_ref, bf_ref, bb_ref, gf_ref, gb_ref, *, t, nblk):
    i = pl.program_id(1)
    h = pl.program_id(2)
    pad = CONV_K // 2

    def conv_silu(x_ref, p_ref, n_ref, w):
        prev = jnp.where(i == 0, 0.0, p_ref[0])
        nxt = jnp.where(i == nblk - 1, 0.0, n_ref[0])
        xp = jnp.concatenate([prev, x_ref[0], nxt], axis=0)
        acc = jnp.zeros((t, LANES), jnp.float32)
        for j in range(CONV_K):
            lo = SUBLANES - pad + j
            acc = acc + xp[lo:lo + t] * w[j:j + 1]
        return _silu(acc)

    def l2n(x):
        return x * lax.rsqrt(jnp.sum(x * x, axis=-1, keepdims=True) + EPS)

    cw = cw_ref[...]
    qs_ref[...] = l2n(conv_silu(q_ref, qp_ref, qn_ref, cw[0])) * (DN_HEAD_DIM ** -0.5)
    ks_ref[...] = l2n(conv_silu(k_ref, kp_ref, kn_ref, cw[1]))
    vs_ref[...] = conv_silu(v_ref, vp_ref, vn_ref, cw[2])

    bf = jnp.bfloat16
    lane = lax.broadcasted_iota(jnp.int32, (1, LANES), 1)

    ba = ba_ref[0]
    g_all = -jnp.exp(alog_ref[...]) * _softplus(ba + dtb_ref[...])
    beta_all = jax.nn.sigmoid(ba)

    def pick(x, idx):
        col = jnp.sum(jnp.where(lane == idx, x, 0.0), axis=-1, keepdims=True)
        return jnp.broadcast_to(col, (t, LANES))

    bf_ref[...] = pick(beta_all, h)
    bb_ref[...] = pick(beta_all, DN_HEADS + h)
    gf_ref[...] = pick(g_all, 2 * DN_HEADS + h)
    gb_ref[...] = pick(g_all, 3 * DN_HEADS + h)

    row = lax.broadcasted_iota(jnp.int32, (CHUNK, 2 * CHUNK), 0)
    col2 = lax.broadcasted_iota(jnp.int32, (CHUNK, 2 * CHUNK), 1)
    left = col2 < CHUNK
    colm = col2 % CHUNK
    keep2 = (left & (row >= colm)) | (~left & (row <= colm))
    strict2 = keep2 & (row != colm)
    diag2 = row == colm
    eye2 = diag2.astype(jnp.float32)
    same2 = (row // INV_BLOCK) == (colm // INV_BLOCK)
    row4 = lax.broadcasted_iota(jnp.int32, (2 * CHUNK, 2 * CHUNK), 0)
    col4 = lax.broadcasted_iota(jnp.int32, (2 * CHUNK, 2 * CHUNK), 1)
    bd_mask = ((row4 < CHUNK) == (col4 < CHUNK)).astype(bf)
    r3 = lax.broadcasted_iota(jnp.int32, (CHUNK, 3 * CHUNK), 0)
    c3 = lax.broadcasted_iota(jnp.int32, (CHUNK, 3 * CHUNK), 1) % CHUNK
    tril3 = (r3 >= c3).astype(bf)
    triu3 = (r3 <= c3).astype(bf)

    def blockdiag(y):
        yb = y.astype(bf)
        return jnp.concatenate([yb, yb], axis=0) * bd_mask

    def cumsum(tri3, g):
        hi, mid, lo = _split3(g)
        return _dot(tri3, jnp.concatenate([hi, mid, lo], axis=0).astype(bf))

    def st_load(c):
        rows = pl.ds(pl.multiple_of(c * CHUNK, CHUNK), CHUNK)
        v = dict(c=c, rows=rows, qc=qs_ref[rows, :], kc=ks_ref[rows, :], vc=vs_ref[rows, :],
                 beta_f=bf_ref[rows, :], beta_b=bb_ref[rows, :], g_f=gf_ref[rows, :], g_b=gb_ref[rows, :])
        kcb = v['kc'].astype(bf)
        kk = jnp.concatenate([kcb, kcb], axis=0)
        v['gram2'] = _dot_nt(kcb, kk)
        v['qk2'] = _dot_nt(v['qc'].astype(bf), kk)
        v['gc_f'] = cumsum(tril3, v['g_f'])
        v['gc_b'] = cumsum(triu3, v['g_b'])
        return v

    def st_lmat(v):
        gc2 = jnp.where(left, v['gc_f'], v['gc_b'])
        gc_row = jnp.sum(jnp.where(diag2, gc2, 0.0), axis=0, keepdims=True)
        decay2 = jnp.where(keep2, jnp.exp(jnp.where(keep2, gc2 - gc_row, 0.0)), 0.0)
        l2 = jnp.where(strict2, jnp.where(left, v['beta_f'], v['beta_b']) * v['gram2'] * decay2, 0.0)
        ld = jnp.where(same2, l2, 0.0)
        v.update(qkm=(v['qk2'] * decay2).astype(bf), lo=l2 - ld, p=eye2 - ld,
                 m=_dot(ld.astype(bf), blockdiag(ld)))
        del v['gram2'], v['qk2']
        return v

    def st_level(v):
        r = _dot(jnp.concatenate([v['p'], v['m']], axis=0).astype(bf), blockdiag(v['m']))
        v.update(p=v['p'] + r[:CHUNK], m=r[CHUNK:])
        return v

    def st_dinv(v):
        v['d_inv'] = v['p'] + _dot(v['p'].astype(bf), blockdiag(v['m']))
        return v

    def st_n(v):
        v['n'] = _dot(v['d_inv'].astype(bf), blockdiag(v['lo']))
        return v

    def st_nsq(v):
        v['nsq'] = _dot(v['n'].astype(bf), blockdiag(v['n']))
        return v

    def st_pn(v):
        pn = eye2 - v['n']
        v['pn'] = pn + _dot(pn.astype(bf), blockdiag(v['nsq']))
        return v

    def st_t(v):
        v['t2'] = _dot(v['pn'].astype(bf), blockdiag(v['d_inv']))
        return v

    def st_uw(v):
        kc, vc = v['kc'], v['vc']
        v['egc_f'] = jnp.exp(v['gc_f'])
        v['egc_b'] = jnp.exp(v['gc_b'])
        rhs = jnp.concatenate([
            jnp.concatenate([vc * v['beta_f'], kc * (v['beta_f'] * v['egc_f'])], axis=1),
            jnp.concatenate([vc * v['beta_b'], kc * (v['beta_b'] * v['egc_b'])], axis=1)], axis=0).astype(bf)
        v['uw'] = _dot(blockdiag(v['t2']), rhs)
        return v

    def st_store(v):
        c, rows, uw, qc, kc = v['c'], v['rows'], v['uw'], v['qc'], v['kc']
        r2 = pl.multiple_of(c * 2 * CHUNK, 2 * CHUNK)
        for d, egc in ((0, v['egc_f']), (1, v['egc_b'])):
            u_ref[d, 0, 0, rows, :] = uw[d * CHUNK:(d + 1) * CHUNK, :LANES]
            wq_ref[d, 0, 0, pl.ds(r2, CHUNK), :] = uw[d * CHUNK:(d + 1) * CHUNK, LANES:].astype(bf)
            wq_ref[d, 0, 0, pl.ds(r2 + CHUNK, CHUNK), :] = (qc * egc).astype(bf)
        gtot_f = jnp.sum(v['g_f'], axis=0, keepdims=True)
        gtot_b = jnp.sum(v['g_b'], axis=0, keepdims=True)
        kd = jnp.concatenate([kc * jnp.exp(gtot_f - v['gc_f']), kc * jnp.exp(gtot_b - v['gc_b'])], axis=0)
        r3_ = pl.multiple_of(c * 3 * CHUNK, CHUNK)
        qkkd_ref[0, 0, pl.ds(r3_, CHUNK), :] = v['qkm']
        qkkd_ref[0, 0, pl.ds(r3_ + CHUNK, 2 * CHUNK), :] = kd.T.astype(bf)
        r8 = pl.multiple_of(c * SUBLANES, SUBLANES)
        gl_ref[0, 0, 0, pl.ds(r8, SUBLANES), :] = jnp.broadcast_to(jnp.exp(gtot_f), (SUBLANES, LANES))
        gl_ref[1, 0, 0, pl.ds(r8, SUBLANES), :] = jnp.broadcast_to(jnp.exp(gtot_b), (SUBLANES, LANES))

    levels = [st_level] * (int(math.log2(INV_BLOCK)) - 2)
    stages = [st_lmat] + levels + [st_dinv, st_n, st_nsq, st_pn, st_t, st_uw]

    def trip(j, carry):
        vals = [st_load(j * GDN_UNROLL + k) for k in range(GDN_UNROLL)]
        for stage in stages:
            vals = [stage(v) for v in vals]
        for v in vals:
            st_store(v)
        return carry

    lax.fori_loop(0, t // (CHUNK * GDN_UNROLL), trip, 0)


def _gdn_prep(proj, conv_w, a_log_row, dt_bias_row, t):
    b, s, ncol = proj.shape
    nh = DN_HEADS
    nblk = s // t
    n = s // CHUNK
    last_col = ncol // LANES - 1
    tpb = t // SUBLANES

    def main(base):
        return pl.BlockSpec((1, t, LANES), lambda bi, i, hi: (bi, i, base + hi))

    def prev(base):
        return pl.BlockSpec((1, SUBLANES, LANES),
                            lambda bi, i, hi: (bi, jnp.maximum(i * tpb - 1, 0), base + hi))

    def nxt(base):
        return pl.BlockSpec((1, SUBLANES, LANES),
                            lambda bi, i, hi: (bi, jnp.minimum((i + 1) * tpb, s // SUBLANES - 1), base + hi))

    in_specs = []
    for base in (COL_DN_Q, COL_DN_K, COL_DN_V):
        in_specs += [main(base), prev(base), nxt(base)]
    in_specs += [
        pl.BlockSpec((3, CONV_K, LANES), lambda bi, i, hi: (0, 0, hi)),
        pl.BlockSpec((1, t, LANES), lambda bi, i, hi: (bi, i, last_col)),
        pl.BlockSpec((1, LANES), lambda bi, i, hi: (0, 0)),
        pl.BlockSpec((1, LANES), lambda bi, i, hi: (0, 0)),
    ]
    out_specs = [
        pl.BlockSpec((2, 1, 1, 2 * t, LANES), lambda bi, i, hi: (0, bi, hi, i, 0)),
        pl.BlockSpec((2, 1, 1, t, LANES), lambda bi, i, hi: (0, bi, hi, i, 0)),
        pl.BlockSpec((1, 1, 3 * t, LANES), lambda bi, i, hi: (bi, hi, i, 0)),
        pl.BlockSpec((2, 1, 1, t // CHUNK * SUBLANES, LANES), lambda bi, i, hi: (0, bi, hi, i, 0)),
    ]
    out_shape = [
        jax.ShapeDtypeStruct((2, b, nh, 2 * s, LANES), jnp.bfloat16),
        jax.ShapeDtypeStruct((2, b, nh, s, LANES), jnp.float32),
        jax.ShapeDtypeStruct((b, nh, 3 * s, LANES), jnp.bfloat16),
        jax.ShapeDtypeStruct((2, b, nh, n * SUBLANES, LANES), jnp.float32),
    ]
    proj9 = [proj] * 9
    return pl.pallas_call(
        functools.partial(_gdn_prep_kernel, t=t, nblk=nblk),
        grid=(b, nblk, nh),
        in_specs=in_specs,
        out_specs=out_specs,
        out_shape=out_shape,
        scratch_shapes=[pltpu.VMEM((t, LANES), jnp.float32)] * 7,
        compiler_params=_params("parallel", "parallel", "parallel"),
        name="gdn_prep",
    )(*proj9, conv_w, proj, a_log_row, dt_bias_row)


def _gdn_scan_kernel(wq_f, u_f, qkkd_f, gl_f, wq_b, u_b, qkkd_b, gl_b, o_f, o_b, state_ref, *, t):
    nc = t // CHUNK

    @pl.when(pl.program_id(1) == 0)
    def _():
        state_ref[...] = jnp.zeros_like(state_ref)

    streams = ((0, wq_f, u_f, qkkd_f, gl_f, o_f), (1, wq_b, u_b, qkkd_b, gl_b, o_b))
    zeros = jnp.zeros((CHUNK, DN_HEAD_DIM), jnp.bfloat16)

    def body(c, carry):
        work = []
        for d, wq, u, qkkd, gl, o in streams:
            cc = c if d == 0 else nc - 1 - c
            r2 = pl.multiple_of(cc * 2 * CHUNK, 2 * CHUNK)
            for hh in range(DN_HEADS):
                st = state_ref[d, hh]
                ws = _dot(wq[0, 0, hh, pl.ds(r2, 2 * CHUNK), :], st.astype(jnp.bfloat16))
                work.append((d, hh, cc, u, qkkd, gl, o, st, ws))
        for d, hh, cc, u, qkkd, gl, o, st, ws in work:
            r0 = pl.multiple_of(cc * CHUNK, CHUNK)
            r3 = pl.multiple_of(cc * 3 * CHUNK, CHUNK)
            r8 = pl.multiple_of(cc * SUBLANES, SUBLANES)
            v_new = (u[0, 0, hh, pl.ds(r0, CHUNK), :] - ws[:CHUNK]).astype(jnp.bfloat16)
            rhs = jnp.concatenate([v_new, zeros] if d == 0 else [zeros, v_new], axis=0)
            r = _dot(qkkd[0, hh, pl.ds(r3, 3 * CHUNK), :], rhs)
            o[0, 0, hh, pl.ds(r0, CHUNK), :] = ws[CHUNK:] + r[:CHUNK]
            decay = gl[0, 0, hh, pl.ds(r8, SUBLANES), :][0:1]
            state_ref[d, hh] = st * decay + r[CHUNK:]
        return carry

    lax.fori_loop(0, nc, body, 0)


def _gdn_scan(wq, u, qkkd, gl, t):
    _, b, nh, s, _ = u.shape
    nblk = s // t
    g8 = t // CHUNK * SUBLANES

    def specs(d):
        if d == 0:
            idx = lambda bi, i: (0, bi, 0, i, 0)
            idx4 = lambda bi, i: (bi, 0, i, 0)
        else:
            idx = lambda bi, i: (1, bi, 0, nblk - 1 - i, 0)
            idx4 = lambda bi, i: (bi, 0, nblk - 1 - i, 0)
        return [
            pl.BlockSpec((1, 1, nh, 2 * t, LANES), idx),
            pl.BlockSpec((1, 1, nh, t, LANES), idx),
            pl.BlockSpec((1, nh, 3 * t, LANES), idx4),
            pl.BlockSpec((1, 1, nh, g8, LANES), idx),
        ], pl.BlockSpec((1, 1, nh, t, LANES), idx)

    in_f, out_f = specs(0)
    in_b, out_b = specs(1)
    o_shape = jax.ShapeDtypeStruct((2, b, nh, s, LANES), jnp.float32)
    return pl.pallas_call(
        functools.partial(_gdn_scan_kernel, t=t),
        grid=(b, nblk),
        in_specs=in_f + in_b,
        out_specs=[out_f, out_b],
        out_shape=[o_shape, o_shape],
        scratch_shapes=[pltpu.VMEM((2, nh, DN_HEAD_DIM, DN_HEAD_DIM), jnp.float32)],
        compiler_params=_params("parallel", "arbitrary"),
        name="gdn_scan",
    )(wq, u, qkkd, gl, wq, u, qkkd, gl)


def _merge_kernel(of_ref, ob_ref, z_ref, ga_ref, gb_ref, ya_ref, x_ref, gt_ref, og_ref,
                  wa_ref, wb_ref, wo_ref, o_ref):
    og = og_ref[...]
    z = z_ref[0]
    parts = []
    for hh in range(DN_HEADS):
        o = of_ref[0, 0, hh] + ob_ref[0, 0, hh]
        y = o * lax.rsqrt(jnp.mean(o * o, axis=-1, keepdims=True) + EPS) * og
        parts.append((y * _silu(z[:, hh * LANES:(hh + 1) * LANES])).astype(jnp.bfloat16))
    yb = jnp.concatenate(parts, axis=1)
    pa = _dot(ya_ref[0].astype(jnp.bfloat16), wa_ref[...])
    pb = _dot(yb, wb_ref[...])
    merged = jax.nn.sigmoid(ga_ref[0]) * pa + jax.nn.sigmoid(gb_ref[0]) * pb
    o_ref[0] = x_ref[0] + gt_ref[0] * _dot(merged.astype(jnp.bfloat16), wo_ref[...])


def _merge(o_f, o_b, proj, y_a, x, gate, out_gain, w_a, w_b, w_o, tm):
    b, s, d = x.shape
    nh = DN_HEADS
    zw = DN_WIDTH // LANES
    return pl.pallas_call(
        _merge_kernel,
        grid=(b, s // tm),
        in_specs=[
            pl.BlockSpec((1, 1, nh, tm, LANES), lambda bi, i: (0, bi, 0, i, 0)),
            pl.BlockSpec((1, 1, nh, tm, LANES), lambda bi, i: (1, bi, 0, i, 0)),
            pl.BlockSpec((1, tm, DN_WIDTH), lambda bi, i: (bi, i, COL_DN_Z // zw)),
            pl.BlockSpec((1, tm, d), lambda bi, i: (bi, i, 0)),
            pl.BlockSpec((1, tm, d), lambda bi, i: (bi, i, 1)),
            pl.BlockSpec((1, tm, DA_WIDTH), lambda bi, i: (bi, i, 0)),
            pl.BlockSpec((1, tm, d), lambda bi, i: (bi, i, 0)),
            pl.BlockSpec((1, 1, d), lambda bi, i: (bi, 0, 0)),
            pl.BlockSpec((1, LANES), lambda bi, i: (0, 0)),
            pl.BlockSpec(w_a.shape, lambda bi, i: (0, 0)),
            pl.BlockSpec(w_b.shape, lambda bi, i: (0, 0)),
            pl.BlockSpec(w_o.shape, lambda bi, i: (0, 0)),
        ],
        out_specs=pl.BlockSpec((1, tm, d), lambda bi, i: (bi, i, 0)),
        out_shape=jax.ShapeDtypeStruct((b, s, d), jnp.float32),
        compiler_params=_params("parallel", "parallel"),
        name="merge",
    )(o_f, o_b, proj, proj, proj, y_a, x, gate, out_gain, w_a, w_b, w_o)


def _mlp_kernel(x_ref, g_ref, sc_ref, sh_ref, gt_ref, w1_ref, w2_ref, o_ref, h_ref, acc_ref):
    j = pl.program_id(2)

    @pl.when(j == 0)
    def _():
        h_ref[...] = _modulated_norm(x_ref[0], g_ref[...], sc_ref[0], sh_ref[0]).astype(h_ref.dtype)
        acc_ref[...] = jnp.zeros_like(acc_ref)

    a = jnp.maximum(_dot(h_ref[...], w1_ref[...]), 0.0)
    acc_ref[...] += _dot((a * a).astype(jnp.bfloat16), w2_ref[...])

    @pl.when(j == pl.num_programs(2) - 1)
    def _():
        o_ref[0] = x_ref[0] + gt_ref[0] * acc_ref[...]


def _mlp(x, gain, scale, shift, gate, w1, w2, tm, tf):
    b, s, d = x.shape
    f = w1.shape[1]
    vec = pl.BlockSpec((1, 1, d), lambda bi, i, j: (bi, 0, 0))
    return pl.pallas_call(
        _mlp_kernel,
        grid=(b, s // tm, f // tf),
        in_specs=[
            pl.BlockSpec((1, tm, d), lambda bi, i, j: (bi, i, 0)),
            pl.BlockSpec((1, d), lambda bi, i, j: (0, 0)),
            vec, vec, vec,
            pl.BlockSpec((d, tf), lambda bi, i, j: (0, j)),
            pl.BlockSpec((tf, d), lambda bi, i, j: (j, 0)),
        ],
        out_specs=pl.BlockSpec((1, tm, d), lambda bi, i, j: (bi, i, 0)),
        out_shape=jax.ShapeDtypeStruct((b, s, d), jnp.float32),
        scratch_shapes=[pltpu.VMEM((tm, d), jnp.bfloat16), pltpu.VMEM((tm, d), jnp.float32)],
        compiler_params=_params("parallel", "parallel", "arbitrary"),
        name="mlp",
    )(x, gain, scale, shift, gate, w1, w2)


def _reorder_w_in(w_in, d_model):
    assert d_model == D_MODEL
    main = 3 * DA_WIDTH + 4 * DN_WIDTH
    small = 4 * DN_HEADS
    head = w_in[:, :main]
    ba = w_in[:, main:main + small]
    gates = w_in[:, main + small:]
    pad = jnp.zeros((w_in.shape[0], LANES - small), w_in.dtype)
    return jnp.concatenate([gates, head, ba, pad], axis=1).astype(jnp.bfloat16)


def _rope_tables(seq):
    pos = jnp.arange(seq, dtype=jnp.float32)
    inv = 1.0 / (ROPE_THETA ** (jnp.arange(0, DA_HEAD_DIM, 2, dtype=jnp.float32) / DA_HEAD_DIM))
    ang = pos[:, None] * inv[None, :]
    ang = jnp.concatenate([ang, ang, ang, ang], axis=-1)
    sign = jnp.where((jnp.arange(LANES) % DA_HEAD_DIM) < DA_HEAD_DIM // 2, -1.0, 1.0)
    return jnp.cos(ang), jnp.sin(ang) * sign


def kernel(x, c, ada_w, ada_b, norm1_g, norm2_g, w_in, dn_conv, dn_A_log, dn_dt_bias, dn_out_gain,
           qk_norm_gain, diff_lambda, diff_subln_gain, w_branch_a, w_branch_b, w_out, w_mlp1, w_mlp2):
    b, s, d = x.shape
    depth = ada_w.shape[0]
    assert s % 512 == 0 and d % LANES == 0 and b <= SUBLANES
    tm = min(1024, s)
    t_gdn = min(512, s)
    tq = min(256, s)
    tk = min(512, s // ATTN_UNROLL)

    c_pad = jnp.zeros((SUBLANES, d), jnp.float32).at[:b].set(c)
    mod = _adaln(c_pad, ada_w, ada_b)[:, :b]
    cos, sin_signed = _rope_tables(s)
    small = 4 * DN_HEADS

    for l in range(depth):
        sh1, sc1, gt1, sh2, sc2, gt2 = [mod[l, :, None, k * d:(k + 1) * d] for k in range(6)]
        lambda_init = 0.8 - 0.6 * math.exp(-0.3 * l)

        w_in_l = _reorder_w_in(w_in[l], d)
        proj = _inproj(x, norm1_g[l][None], sc1, sh1, w_in_l, tm, 1152)

        qgain = jnp.tile(qk_norm_gain[l, 0], 2)[None]
        kgain = jnp.tile(qk_norm_gain[l, 1], 2)[None]
        qs, kt, v = _attn_prep(proj, cos, sin_signed, qgain, kgain, min(512, s))
        y_a = _attn(qs, kt, v, diff_lambda[l], diff_subln_gain[l][None], lambda_init, tq, tk)

        conv_w = dn_conv[l].reshape(CONV_K, 3, DN_WIDTH).transpose(1, 0, 2)
        row = jnp.zeros((1, LANES), jnp.float32)
        a_log_row = row.at[0, small // 2:small].set(dn_A_log[l].reshape(-1))
        dt_bias_row = row.at[0, small // 2:small].set(dn_dt_bias[l].reshape(-1))
        wq, u, qkkd, gl = _gdn_prep(proj, conv_w, a_log_row, dt_bias_row, t_gdn)
        o_f, o_b = _gdn_scan(wq, u, qkkd, gl, t_gdn)

        x = _merge(o_f, o_b, proj, y_a, x, gt1, dn_out_gain[l][None],
                   w_branch_a[l].astype(jnp.bfloat16), w_branch_b[l].astype(jnp.bfloat16),
                   w_out[l].astype(jnp.bfloat16), min(512, s))
        x = _mlp(x, norm2_g[l][None], sc2, sh2, gt2,
                 w_mlp1[l].astype(jnp.bfloat16), w_mlp2[l].astype(jnp.bfloat16), tm, 1024)
    return x
```

```python
import functools
import math

import jax
import jax.numpy as jnp
from jax import lax
from jax.experimental import pallas as pl
from jax.experimental.pallas import tpu as pltpu

DA_HEADS = 4
DA_HEAD_DIM = 64
DA_WIDTH = DA_HEADS * 2 * DA_HEAD_DIM
DN_HEADS = 4
DN_HEAD_DIM = 128
DN_WIDTH = DN_HEADS * DN_HEAD_DIM
CONV_K = 5
CHUNK = 64
ROPE_THETA = 10000.0
EPS = 1e-6
LOG2_E = math.log2(math.e)

LANES = 128
SUBLANES = 8
VMEM_LIMIT_BYTES = 56 * 1024 * 1024

D_MODEL = 1024
COL_GATE = 0
COL_DA_Q = 2 * D_MODEL // LANES
COL_DA_K = COL_DA_Q + DA_WIDTH // LANES
COL_DA_V = COL_DA_K + DA_WIDTH // LANES
COL_DN_Q = COL_DA_V + DA_WIDTH // LANES
COL_DN_K = COL_DN_Q + DN_WIDTH // LANES
COL_DN_V = COL_DN_K + DN_WIDTH // LANES
COL_DN_Z = COL_DN_V + DN_WIDTH // LANES


def _params(*sem):
    return pltpu.CompilerParams(dimension_semantics=sem, vmem_limit_bytes=VMEM_LIMIT_BYTES)


def _silu(x):
    return x * jax.nn.sigmoid(x)


def _softplus(x):
    return jnp.maximum(x, 0.0) + jnp.log(1.0 + jnp.exp(-jnp.abs(x)))


def _dot(a, b):
    return jnp.dot(a, b, preferred_element_type=jnp.float32)


def _dot_nt(a, b):
    return lax.dot_general(a, b, (((1,), (1,)), ((), ())), preferred_element_type=jnp.float32)


def _split3(x):
    hi = x.astype(jnp.bfloat16).astype(jnp.float32)
    r1 = x - hi
    mid = r1.astype(jnp.bfloat16).astype(jnp.float32)
    lo = (r1 - mid).astype(jnp.bfloat16).astype(jnp.float32)
    return hi, mid, lo


def _adaln_kernel(c_ref, w_ref, b_ref, o_ref):
    c_act = _silu(c_ref[...])
    o_ref[0] = jnp.dot(c_act, w_ref[0], preferred_element_type=jnp.float32,
                       precision=lax.Precision.HIGHEST) + b_ref[0]


def _adaln(c_pad, ada_w, ada_b):
    depth, d, n = ada_w.shape
    tn = 1536
    return pl.pallas_call(
        _adaln_kernel,
        grid=(depth, n // tn),
        in_specs=[
            pl.BlockSpec((SUBLANES, d), lambda l, j: (0, 0)),
            pl.BlockSpec((1, d, tn), lambda l, j: (l, 0, j)),
            pl.BlockSpec((1, 1, tn), lambda l, j: (l, 0, j)),
        ],
        out_specs=pl.BlockSpec((1, SUBLANES, tn), lambda l, j: (l, 0, j)),
        out_shape=jax.ShapeDtypeStruct((depth, SUBLANES, n), jnp.float32),
        compiler_params=_params("parallel", "parallel"),
        name="adaln",
    )(c_pad, ada_w, ada_b.reshape(depth, 1, n))


def _modulated_norm(x, gain, scale, shift):
    y = x * lax.rsqrt(jnp.mean(x * x, axis=-1, keepdims=True) + EPS)
    return (y * gain) * (1.0 + scale) + shift


INPROJ_COLS = 512


def _inproj_kernel(x_ref, g_ref, sc_ref, sh_ref, w_ref, wba_ref, o_ref, ba_ref):
    h = _modulated_norm(x_ref[0], g_ref[...], sc_ref[0], sh_ref[0]).astype(jnp.bfloat16)
    for c in range(0, w_ref.shape[1], INPROJ_COLS):
        o_ref[0, :, c:c + INPROJ_COLS] = _dot(h, w_ref[:, c:c + INPROJ_COLS]).astype(o_ref.dtype)
    ba_ref[0] = _dot(h, wba_ref[...])


def _inproj(x, gain, scale, shift, w, w_ba, tm):
    b, s, d = x.shape
    n = w.shape[1]
    assert n % INPROJ_COLS == 0
    resident = dict(pipeline_mode=pl.Buffered(1))
    return pl.pallas_call(
        _inproj_kernel,
        grid=(b, s // tm),
        in_specs=[
            pl.BlockSpec((1, tm, d), lambda bi, i: (bi, i, 0)),
            pl.BlockSpec((1, d), lambda bi, i: (0, 0)),
            pl.BlockSpec((1, 1, d), lambda bi, i: (bi, 0, 0)),
            pl.BlockSpec((1, 1, d), lambda bi, i: (bi, 0, 0)),
            pl.BlockSpec((d, n), lambda bi, i: (0, 0), **resident),
            pl.BlockSpec((d, LANES), lambda bi, i: (0, 0), **resident),
        ],
        out_specs=[pl.BlockSpec((1, tm, n), lambda bi, i: (bi, i, 0)),
                   pl.BlockSpec((1, tm, LANES), lambda bi, i: (bi, i, 0))],
        out_shape=[jax.ShapeDtypeStruct((b, s, n), jnp.bfloat16),
                   jax.ShapeDtypeStruct((b, s, LANES), jnp.float32)],
        compiler_params=_params("parallel", "parallel"),
        name="inproj",
    )(x, gain, scale, shift, w, w_ba)


def _attn_prep_kernel(q_ref, k_ref, v_ref, cos_ref, sin_ref, qg_ref, kg_ref, qo_ref, kt_ref, vo_ref):
    lane = lax.broadcasted_iota(jnp.int32, (1, LANES), 1)
    left = lane < DA_HEAD_DIM
    first_half = (lane % DA_HEAD_DIM) < (DA_HEAD_DIM // 2)
    cos = cos_ref[...]
    sin = sin_ref[...]

    def norm_rope(x, gain):
        sq = x * x
        s_all = jnp.sum(sq, axis=-1, keepdims=True)
        s_left = jnp.sum(jnp.where(left, sq, 0.0), axis=-1, keepdims=True)
        ms = jnp.where(left, s_left, s_all - s_left) * (1.0 / DA_HEAD_DIM)
        y = x * lax.rsqrt(ms + EPS) * gain
        fwd = pltpu.roll(y, LANES - DA_HEAD_DIM // 2, axis=1)
        bwd = pltpu.roll(y, DA_HEAD_DIM // 2, axis=1)
        return y * cos + jnp.where(first_half, fwd, bwd) * sin

    q = norm_rope(q_ref[0].astype(jnp.float32), qg_ref[...]) * (DA_HEAD_DIM ** -0.5 * LOG2_E)
    k = norm_rope(k_ref[0].astype(jnp.float32), kg_ref[...])
    qo_ref[0, 0, 0] = jnp.where(left, q, 0.0).astype(qo_ref.dtype)
    qo_ref[0, 0, 1] = jnp.where(left, 0.0, q).astype(qo_ref.dtype)
    kt_ref[0, 0] = k.T.astype(kt_ref.dtype)
    v = v_ref[0].astype(vo_ref.dtype)
    vo_ref[0, 0] = jnp.concatenate([v, jnp.ones_like(v)], axis=1)


def _attn_prep(proj, cos, sin_signed, qgain, kgain, ts):
    b, s, _ = proj.shape
    h = DA_HEADS
    col = lambda base: (lambda bi, hi, i: (bi, i, base + hi))
    return pl.pallas_call(
        _attn_prep_kernel,
        grid=(b, h, s // ts),
        in_specs=[
            pl.BlockSpec((1, ts, LANES), col(COL_DA_Q)),
            pl.BlockSpec((1, ts, LANES), col(COL_DA_K)),
            pl.BlockSpec((1, ts, LANES), col(COL_DA_V)),
            pl.BlockSpec((ts, LANES), lambda bi, hi, i: (i, 0)),
            pl.BlockSpec((ts, LANES), lambda bi, hi, i: (i, 0)),
            pl.BlockSpec((1, LANES), lambda bi, hi, i: (0, 0)),
            pl.BlockSpec((1, LANES), lambda bi, hi, i: (0, 0)),
        ],
        out_specs=[
            pl.BlockSpec((1, 1, 2, ts, LANES), lambda bi, hi, i: (bi, hi, 0, i, 0)),
            pl.BlockSpec((1, 1, LANES, ts), lambda bi, hi, i: (bi, hi, 0, i)),
            pl.BlockSpec((1, 1, ts, 2 * LANES), lambda bi, hi, i: (bi, hi, i, 0)),
        ],
        out_shape=[
            jax.ShapeDtypeStruct((b, h, 2, s, LANES), jnp.bfloat16),
            jax.ShapeDtypeStruct((b, h, LANES, s), jnp.bfloat16),
            jax.ShapeDtypeStruct((b, h, s, 2 * LANES), jnp.bfloat16),
        ],
        compiler_params=_params("parallel", "parallel", "parallel"),
        name="attn_prep",
    )(proj, proj, proj, cos, sin_signed, qgain, kgain)


ATTN_UNROLL = 8


ATTN_ROW_BLOCK = 32


def _attn_kernel(q_ref, kt_ref, v_ref, lam_ref, g_ref, o_ref, s_ref, p_ref, m_ref, alpha_ref, acc_ref,
                 *, tq, tk, lambda_init):
    s_len = v_ref.shape[2]
    q = q_ref[0, 0].reshape(2 * tq, LANES)
    n_kv = s_len // tk
    groups = tk // LANES
    rb = ATTN_ROW_BLOCK

    m_ref[...] = jnp.full(m_ref.shape, -jnp.inf, jnp.float32)
    acc_ref[...] = jnp.zeros(acc_ref.shape, jnp.float32)

    def scores(j, slot):
        s_ref[slot] = _dot(q, kt_ref[0, 0, :, pl.ds(pl.multiple_of(j * tk, tk), tk)])

    def update(j, slot):
        for r in range(2 * tq // rb):
            rows = slice(r * rb, (r + 1) * rb)
            sg = [s_ref[slot, rows, g * LANES:(g + 1) * LANES] for g in range(groups)]
            m_old = m_ref[rows, :]
            m_new = jnp.maximum(m_old, jnp.max(functools.reduce(jnp.maximum, sg), axis=-1, keepdims=True))
            alpha_ref[rows, :] = jnp.exp2(m_old - m_new)
            m_ref[rows, :] = m_new
            for g in range(groups):
                p_ref[rows, g * LANES:(g + 1) * LANES] = jnp.exp2(sg[g] - m_new).astype(jnp.bfloat16)
        alpha = alpha_ref[...]
        pv = _dot(p_ref[...], v_ref[0, 0, pl.ds(pl.multiple_of(j * tk, tk), tk), :])
        acc_ref[...] = jnp.concatenate([alpha, alpha], axis=1) * acc_ref[...] + pv

    def body(jj, carry):
        j = ATTN_UNROLL * jj
        for k in range(ATTN_UNROLL):
            scores(jnp.minimum(j + k + 1, n_kv - 1), (k + 1) % 2)
            update(j + k, k % 2)
        return carry

    scores(0, 0)
    lax.fori_loop(0, n_kv // ATTN_UNROLL, body, 0)
    o = acc_ref[:, :LANES] / acc_ref[:, LANES:]
    lq = lam_ref[...]
    lam = (jnp.exp(jnp.sum(lq[0:1] * lq[1:2], axis=-1, keepdims=True))
           - jnp.exp(jnp.sum(lq[2:3] * lq[3:4], axis=-1, keepdims=True)) + lambda_init)
    d = o[:tq] - lam * o[tq:]
    y = d * lax.rsqrt(jnp.mean(d * d, axis=-1, keepdims=True) + EPS) * g_ref[...]
    o_ref[0] = (y * (1.0 - lambda_init)).astype(o_ref.dtype)


def _attn(qs, kt, v, lam_params, subln_gain, lambda_init, tq, tk):
    b, h, _, s, _ = qs.shape
    assert s % (tk * ATTN_UNROLL) == 0 and ATTN_UNROLL % 2 == 0 and (2 * tq) % ATTN_ROW_BLOCK == 0
    return pl.pallas_call(
        functools.partial(_attn_kernel, tq=tq, tk=tk, lambda_init=lambda_init),
        grid=(b, h, s // tq),
        in_specs=[
            pl.BlockSpec((1, 1, 2, tq, LANES), lambda bi, hi, i: (bi, hi, 0, i, 0)),
            pl.BlockSpec((1, 1, LANES, s), lambda bi, hi, i: (bi, hi, 0, 0)),
            pl.BlockSpec((1, 1, s, 2 * LANES), lambda bi, hi, i: (bi, hi, 0, 0)),
            pl.BlockSpec(lam_params.shape, lambda bi, hi, i: (0, 0)),
            pl.BlockSpec((1, LANES), lambda bi, hi, i: (0, 0)),
        ],
        out_specs=pl.BlockSpec((1, tq, LANES), lambda bi, hi, i: (bi, i, hi)),
        out_shape=jax.ShapeDtypeStruct((b, s, h * LANES), jnp.bfloat16),
        scratch_shapes=[pltpu.VMEM((2, 2 * tq, tk), jnp.float32), pltpu.VMEM((2 * tq, tk), jnp.bfloat16),
                        pltpu.VMEM((2 * tq, LANES), jnp.float32), pltpu.VMEM((2 * tq, LANES), jnp.float32),
                        pltpu.VMEM((2 * tq, 2 * LANES), jnp.float32)],
        compiler_params=_params("parallel", "parallel", "parallel"),
        name="attn",
    )(qs, kt, v, lam_params, subln_gain)


INV_BLOCK = 16
HALO = 16


GDN_UNROLL = 8


def _gdn_prep_kernel(q_ref, qp_ref, qn_ref, k_ref, kp_ref, kn_ref, v_ref, vp_ref, vn_ref,
                     cw_ref, ba_ref, alog_ref, dtb_ref,
                     wq_ref, u_ref, qkkd_ref, gl_ref,
                     qs_ref, ks_ref, vs_ref, bf_ref, bb_ref, gf_ref, gb_ref, xq_ref, xk_ref, xv_ref,
                     *, t, nblk):
    i = pl.program_id(1)
    h = pl.program_id(2)
    pad = CONV_K // 2

    def conv_silu(x_ref, p_ref, n_ref, w, xp_ref):
        f32 = jnp.float32
        xp_ref[0:HALO, :] = jnp.where(i == 0, 0.0, p_ref[0].astype(f32))
        xp_ref[HALO:HALO + t, :] = x_ref[0].astype(f32)
        xp_ref[HALO + t:2 * HALO + t, :] = jnp.where(i == nblk - 1, 0.0, n_ref[0].astype(f32))
        acc = xp_ref[HALO - pad:HALO - pad + t, :] * w[0:1]
        for j in range(1, CONV_K):
            acc = acc + xp_ref[HALO - pad + j:HALO - pad + j + t, :] * w[j:j + 1]
        return _silu(acc)

    def l2n(x):
        return x * lax.rsqrt(jnp.sum(x * x, axis=-1, keepdims=True) + EPS)

    cw = cw_ref[...]
    qs_ref[...] = l2n(conv_silu(q_ref, qp_ref, qn_ref, cw[0], xq_ref)) * (DN_HEAD_DIM ** -0.5)
    ks_ref[...] = l2n(conv_silu(k_ref, kp_ref, kn_ref, cw[1], xk_ref))
    vs_ref[...] = conv_silu(v_ref, vp_ref, vn_ref, cw[2], xv_ref)

    bf = jnp.bfloat16
    lane = lax.broadcasted_iota(jnp.int32, (1, LANES), 1)

    r3 = lax.broadcasted_iota(jnp.int32, (CHUNK, 3 * CHUNK), 0)
    c3 = lax.broadcasted_iota(jnp.int32, (CHUNK, 3 * CHUNK), 1) % CHUNK
    tril3 = (r3 >= c3).astype(bf)
    triu3 = (r3 <= c3).astype(bf)

    ba = ba_ref[0]
    g_all = -jnp.exp(alog_ref[...]) * _softplus(ba + dtb_ref[...])
    beta_all = jax.nn.sigmoid(ba)
    hi, mid, lo = _split3(g_all)
    gcf, gcb = [], []
    for c in range(t // CHUNK):
        rows = slice(c * CHUNK, (c + 1) * CHUNK)
        parts = jnp.concatenate([hi[rows], mid[rows], lo[rows]], axis=0).astype(bf)
        gcf.append(_dot(tril3, parts))
        gcb.append(_dot(triu3, parts))

    def pick(x, idx):
        col = jnp.sum(jnp.where(lane == idx, x, 0.0), axis=-1, keepdims=True)
        return jnp.broadcast_to(col, (t, LANES))

    bf_ref[...] = pick(beta_all, h)
    bb_ref[...] = pick(beta_all, DN_HEADS + h)
    gf_ref[...] = pick(jnp.concatenate(gcf, axis=0), 2 * DN_HEADS + h)
    gb_ref[...] = pick(jnp.concatenate(gcb, axis=0), 3 * DN_HEADS + h)

    row = lax.broadcasted_iota(jnp.int32, (CHUNK, 2 * CHUNK), 0)
    col2 = lax.broadcasted_iota(jnp.int32, (CHUNK, 2 * CHUNK), 1)
    left = col2 < CHUNK
    colm = col2 % CHUNK
    keep2 = (left & (row >= colm)) | (~left & (row <= colm))
    strict2 = keep2 & (row != colm)
    diag2 = row == colm
    eye2 = diag2.astype(jnp.float32)
    same2 = (row // INV_BLOCK) == (colm // INV_BLOCK)
    row4 = lax.broadcasted_iota(jnp.int32, (2 * CHUNK, 2 * CHUNK), 0)
    col4 = lax.broadcasted_iota(jnp.int32, (2 * CHUNK, 2 * CHUNK), 1)
    bd_mask = ((row4 < CHUNK) == (col4 < CHUNK)).astype(bf)

    def blockdiag(y):
        yb = y.astype(bf)
        return jnp.concatenate([yb, yb], axis=0) * bd_mask

    def st_load(c):
        rows = pl.ds(pl.multiple_of(c * CHUNK, CHUNK), CHUNK)
        v = dict(c=c, rows=rows, qc=qs_ref[rows, :], kc=ks_ref[rows, :], vc=vs_ref[rows, :],
                 beta_f=bf_ref[rows, :], beta_b=bb_ref[rows, :], gc_f=gf_ref[rows, :], gc_b=gb_ref[rows, :])
        kcb = v['kc'].astype(bf)
        kk = jnp.concatenate([kcb, kcb], axis=0)
        v['gram2'] = _dot_nt(kcb, kk)
        v['qk2'] = _dot_nt(v['qc'].astype(bf), kk)
        return v

    def st_lmat(v):
        gc2 = jnp.where(left, v['gc_f'], v['gc_b'])
        gc_row = jnp.sum(jnp.where(diag2, gc2, 0.0), axis=0, keepdims=True)
        decay2 = jnp.where(keep2, jnp.exp(jnp.where(keep2, gc2 - gc_row, 0.0)), 0.0)
        l2 = jnp.where(strict2, jnp.where(left, v['beta_f'], v['beta_b']) * v['gram2'] * decay2, 0.0)
        ld = jnp.where(same2, l2, 0.0)
        v.update(qkm=(v['qk2'] * decay2).astype(bf), lo=l2 - ld, p=eye2 - ld,
                 m=_dot(ld.astype(bf), blockdiag(ld)))
        del v['gram2'], v['qk2']
        return v

    def st_level(v):
        r = _dot(jnp.concatenate([v['p'], v['m']], axis=0).astype(bf), blockdiag(v['m']))
        v.update(p=v['p'] + r[:CHUNK], m=r[CHUNK:])
        return v

    def st_dinv(v):
        v['d_inv'] = v['p'] + _dot(v['p'].astype(bf), blockdiag(v['m']))
        return v

    def st_n(v):
        v['n'] = _dot(v['d_inv'].astype(bf), blockdiag(v['lo']))
        return v

    def st_nsq(v):
        v['nsq'] = _dot(v['n'].astype(bf), blockdiag(v['n']))
        return v

    def st_pn(v):
        pn = eye2 - v['n']
        v['pn'] = pn + _dot(pn.astype(bf), blockdiag(v['nsq']))
        return v

    def st_t(v):
        v['t2'] = _dot(v['pn'].astype(bf), blockdiag(v['d_inv']))
        return v

    def st_uw(v):
        kc, vc = v['kc'], v['vc']
        v['egc_f'] = jnp.exp(v['gc_f'])
        v['egc_b'] = jnp.exp(v['gc_b'])
        rhs = jnp.concatenate([
            jnp.concatenate([vc * v['beta_f'], kc * (v['beta_f'] * v['egc_f'])], axis=1),
            jnp.concatenate([vc * v['beta_b'], kc * (v['beta_b'] * v['egc_b'])], axis=1)], axis=0).astype(bf)
        v['uw'] = _dot(blockdiag(v['t2']), rhs)
        return v

    def st_store(v):
        c, rows, uw, qc, kc = v['c'], v['rows'], v['uw'], v['qc'], v['kc']
        r2 = pl.multiple_of(c * 2 * CHUNK, 2 * CHUNK)
        for d, egc in ((0, v['egc_f']), (1, v['egc_b'])):
            u_ref[d, 0, 0, rows, :] = uw[d * CHUNK:(d + 1) * CHUNK, :LANES]
            wq_ref[d, 0, 0, pl.ds(r2, CHUNK), :] = uw[d * CHUNK:(d + 1) * CHUNK, LANES:].astype(bf)
            wq_ref[d, 0, 0, pl.ds(r2 + CHUNK, CHUNK), :] = (qc * egc).astype(bf)
        gtot_f = v['gc_f'][CHUNK - 1:CHUNK, :]
        gtot_b = v['gc_b'][0:1, :]
        kd = jnp.concatenate([kc * jnp.exp(gtot_f - v['gc_f']), kc * jnp.exp(gtot_b - v['gc_b'])], axis=0)
        r3_ = pl.multiple_of(c * 3 * CHUNK, CHUNK)
        qkkd_ref[0, 0, pl.ds(r3_, CHUNK), :] = v['qkm']
        qkkd_ref[0, 0, pl.ds(r3_ + CHUNK, 2 * CHUNK), :] = kd.T.astype(bf)
        r8 = pl.multiple_of(c * SUBLANES, SUBLANES)
        gl_ref[0, 0, 0, pl.ds(r8, SUBLANES), :] = jnp.broadcast_to(jnp.exp(gtot_f), (SUBLANES, LANES))
        gl_ref[1, 0, 0, pl.ds(r8, SUBLANES), :] = jnp.broadcast_to(jnp.exp(gtot_b), (SUBLANES, LANES))

    levels = [st_level] * (int(math.log2(INV_BLOCK)) - 2)
    stages = [st_lmat] + levels + [st_dinv, st_n, st_nsq, st_pn, st_t, st_uw]

    def trip(j, carry):
        vals = [st_load(j * GDN_UNROLL + k) for k in range(GDN_UNROLL)]
        for stage in stages:
            vals = [stage(v) for v in vals]
        for v in vals:
            st_store(v)
        return carry

    lax.fori_loop(0, t // (CHUNK * GDN_UNROLL), trip, 0)


def _gdn_prep(proj, ba, conv_w, a_log_row, dt_bias_row, t):
    b, s, _ = proj.shape
    nh = DN_HEADS
    nblk = s // t
    n = s // CHUNK
    tpb = t // HALO

    def main(base):
        return pl.BlockSpec((1, t, LANES), lambda bi, i, hi: (bi, i, base + hi))

    def prev(base):
        return pl.BlockSpec((1, HALO, LANES),
                            lambda bi, i, hi: (bi, jnp.maximum(i * tpb - 1, 0), base + hi))

    def nxt(base):
        return pl.BlockSpec((1, HALO, LANES),
                            lambda bi, i, hi: (bi, jnp.minimum((i + 1) * tpb, s // HALO - 1), base + hi))

    in_specs = []
    for base in (COL_DN_Q, COL_DN_K, COL_DN_V):
        in_specs += [main(base), prev(base), nxt(base)]
    in_specs += [
        pl.BlockSpec((3, CONV_K, LANES), lambda bi, i, hi: (0, 0, hi)),
        pl.BlockSpec((1, t, LANES), lambda bi, i, hi: (bi, i, 0)),
        pl.BlockSpec((1, LANES), lambda bi, i, hi: (0, 0)),
        pl.BlockSpec((1, LANES), lambda bi, i, hi: (0, 0)),
    ]
    out_specs = [
        pl.BlockSpec((2, 1, 1, 2 * t, LANES), lambda bi, i, hi: (0, bi, hi, i, 0)),
        pl.BlockSpec((2, 1, 1, t, LANES), lambda bi, i, hi: (0, bi, hi, i, 0)),
        pl.BlockSpec((1, 1, 3 * t, LANES), lambda bi, i, hi: (bi, hi, i, 0)),
        pl.BlockSpec((2, 1, 1, t // CHUNK * SUBLANES, LANES), lambda bi, i, hi: (0, bi, hi, i, 0)),
    ]
    out_shape = [
        jax.ShapeDtypeStruct((2, b, nh, 2 * s, LANES), jnp.bfloat16),
        jax.ShapeDtypeStruct((2, b, nh, s, LANES), jnp.float32),
        jax.ShapeDtypeStruct((b, nh, 3 * s, LANES), jnp.bfloat16),
        jax.ShapeDtypeStruct((2, b, nh, n * SUBLANES, LANES), jnp.float32),
    ]
    proj9 = [proj] * 9
    return pl.pallas_call(
        functools.partial(_gdn_prep_kernel, t=t, nblk=nblk),
        grid=(b, nblk, nh),
        in_specs=in_specs,
        out_specs=out_specs,
        out_shape=out_shape,
        scratch_shapes=([pltpu.VMEM((t, LANES), jnp.float32)] * 7
                        + [pltpu.VMEM((t + 2 * HALO, LANES), jnp.float32)] * 3),
        compiler_params=_params("parallel", "parallel", "parallel"),
        name="gdn_prep",
    )(*proj9, conv_w, ba, a_log_row, dt_bias_row)


def _gdn_scan_kernel(wq_f, u_f, qkkd_f, gl_f, wq_b, u_b, qkkd_b, gl_b, o_f, o_b, state_ref, *, t):
    nc = t // CHUNK

    @pl.when(pl.program_id(1) == 0)
    def _():
        state_ref[...] = jnp.zeros_like(state_ref)

    streams = ((0, wq_f, u_f, qkkd_f, gl_f, o_f), (1, wq_b, u_b, qkkd_b, gl_b, o_b))
    zeros = jnp.zeros((CHUNK, DN_HEAD_DIM), jnp.bfloat16)

    def body(c, carry):
        work = []
        for d, wq, u, qkkd, gl, o in streams:
            cc = c if d == 0 else nc - 1 - c
            r2 = pl.multiple_of(cc * 2 * CHUNK, 2 * CHUNK)
            for hh in range(DN_HEADS):
                st = state_ref[d, hh]
                ws = _dot(wq[0, 0, hh, pl.ds(r2, 2 * CHUNK), :], st.astype(jnp.bfloat16))
                work.append((d, hh, cc, u, qkkd, gl, o, st, ws))
        for d, hh, cc, u, qkkd, gl, o, st, ws in work:
            r0 = pl.multiple_of(cc * CHUNK, CHUNK)
            r3 = pl.multiple_of(cc * 3 * CHUNK, CHUNK)
            r8 = pl.multiple_of(cc * SUBLANES, SUBLANES)
            v_new = (u[0, 0, hh, pl.ds(r0, CHUNK), :] - ws[:CHUNK]).astype(jnp.bfloat16)
            rhs = jnp.concatenate([v_new, zeros] if d == 0 else [zeros, v_new], axis=0)
            r = _dot(qkkd[0, hh, pl.ds(r3, 3 * CHUNK), :], rhs)
            o[0, 0, hh, pl.ds(r0, CHUNK), :] = ws[CHUNK:] + r[:CHUNK]
            decay = gl[0, 0, hh, pl.ds(r8, SUBLANES), :][0:1]
            state_ref[d, hh] = st * decay + r[CHUNK:]
        return carry

    lax.fori_loop(0, nc, body, 0)


def _gdn_scan(wq, u, qkkd, gl, t):
    _, b, nh, s, _ = u.shape
    nblk = s // t
    g8 = t // CHUNK * SUBLANES

    def specs(d):
        if d == 0:
            idx = lambda bi, i: (0, bi, 0, i, 0)
            idx4 = lambda bi, i: (bi, 0, i, 0)
        else:
            idx = lambda bi, i: (1, bi, 0, nblk - 1 - i, 0)
            idx4 = lambda bi, i: (bi, 0, nblk - 1 - i, 0)
        return [
            pl.BlockSpec((1, 1, nh, 2 * t, LANES), idx),
            pl.BlockSpec((1, 1, nh, t, LANES), idx),
            pl.BlockSpec((1, nh, 3 * t, LANES), idx4),
            pl.BlockSpec((1, 1, nh, g8, LANES), idx),
        ], pl.BlockSpec((1, 1, nh, t, LANES), idx)

    in_f, out_f = specs(0)
    in_b, out_b = specs(1)
    o_shape = jax.ShapeDtypeStruct((2, b, nh, s, LANES), jnp.float32)
    return pl.pallas_call(
        functools.partial(_gdn_scan_kernel, t=t),
        grid=(b, nblk),
        in_specs=in_f + in_b,
        out_specs=[out_f, out_b],
        out_shape=[o_shape, o_shape],
        scratch_shapes=[pltpu.VMEM((2, nh, DN_HEAD_DIM, DN_HEAD_DIM), jnp.float32)],
        compiler_params=_params("parallel", "arbitrary"),
        name="gdn_scan",
    )(wq, u, qkkd, gl, wq, u, qkkd, gl)


def _merge_kernel(of_ref, ob_ref, z_ref, ga_ref, gb_ref, ya_ref, x_ref, gt_ref, og_ref,
                  wa_ref, wb_ref, wo_ref, o_ref):
    og = og_ref[...]
    z = z_ref[0].astype(jnp.float32)
    parts = []
    for hh in range(DN_HEADS):
        o = of_ref[0, 0, hh] + ob_ref[0, 0, hh]
        y = o * lax.rsqrt(jnp.mean(o * o, axis=-1, keepdims=True) + EPS) * og
        parts.append((y * _silu(z[:, hh * LANES:(hh + 1) * LANES])).astype(jnp.bfloat16))
    yb = jnp.concatenate(parts, axis=1)
    pa = _dot(ya_ref[0], wa_ref[...])
    pb = _dot(yb, wb_ref[...])
    merged = (jax.nn.sigmoid(ga_ref[0].astype(jnp.float32)) * pa
              + jax.nn.sigmoid(gb_ref[0].astype(jnp.float32)) * pb)
    o_ref[0] = x_ref[0] + gt_ref[0] * _dot(merged.astype(jnp.bfloat16), wo_ref[...])


def _merge(o_f, o_b, proj, y_a, x, gate, out_gain, w_a, w_b, w_o, tm):
    b, s, d = x.shape
    nh = DN_HEADS
    zw = DN_WIDTH // LANES
    return pl.pallas_call(
        _merge_kernel,
        grid=(b, s // tm),
        in_specs=[
            pl.BlockSpec((1, 1, nh, tm, LANES), lambda bi, i: (0, bi, 0, i, 0)),
            pl.BlockSpec((1, 1, nh, tm, LANES), lambda bi, i: (1, bi, 0, i, 0)),
            pl.BlockSpec((1, tm, DN_WIDTH), lambda bi, i: (bi, i, COL_DN_Z // zw)),
            pl.BlockSpec((1, tm, d), lambda bi, i: (bi, i, 0)),
            pl.BlockSpec((1, tm, d), lambda bi, i: (bi, i, 1)),
            pl.BlockSpec((1, tm, DA_WIDTH), lambda bi, i: (bi, i, 0)),
            pl.BlockSpec((1, tm, d), lambda bi, i: (bi, i, 0)),
            pl.BlockSpec((1, 1, d), lambda bi, i: (bi, 0, 0)),
            pl.BlockSpec((1, LANES), lambda bi, i: (0, 0)),
            pl.BlockSpec(w_a.shape, lambda bi, i: (0, 0)),
            pl.BlockSpec(w_b.shape, lambda bi, i: (0, 0)),
            pl.BlockSpec(w_o.shape, lambda bi, i: (0, 0)),
        ],
        out_specs=pl.BlockSpec((1, tm, d), lambda bi, i: (bi, i, 0)),
        out_shape=jax.ShapeDtypeStruct((b, s, d), jnp.float32),
        compiler_params=_params("parallel", "parallel"),
        name="merge",
    )(o_f, o_b, proj, proj, proj, y_a, x, gate, out_gain, w_a, w_b, w_o)


def _mlp_kernel(x_ref, g_ref, sc_ref, sh_ref, gt_ref, w1_ref, w2_ref, o_ref, h_ref, acc_ref):
    j = pl.program_id(2)

    @pl.when(j == 0)
    def _():
        h_ref[...] = _modulated_norm(x_ref[0], g_ref[...], sc_ref[0], sh_ref[0]).astype(h_ref.dtype)
        acc_ref[...] = jnp.zeros_like(acc_ref)

    a = jnp.maximum(_dot(h_ref[...], w1_ref[...]), 0.0)
    acc_ref[...] += _dot((a * a).astype(jnp.bfloat16), w2_ref[...])

    @pl.when(j == pl.num_programs(2) - 1)
    def _():
        o_ref[0] = x_ref[0] + gt_ref[0] * acc_ref[...]


def _mlp(x, gain, scale, shift, gate, w1, w2, tm, tf):
    b, s, d = x.shape
    f = w1.shape[1]
    vec = pl.BlockSpec((1, 1, d), lambda bi, i, j: (bi, 0, 0))
    return pl.pallas_call(
        _mlp_kernel,
        grid=(b, s // tm, f // tf),
        in_specs=[
            pl.BlockSpec((1, tm, d), lambda bi, i, j: (bi, i, 0)),
            pl.BlockSpec((1, d), lambda bi, i, j: (0, 0)),
            vec, vec, vec,
            pl.BlockSpec((d, tf), lambda bi, i, j: (0, j)),
            pl.BlockSpec((tf, d), lambda bi, i, j: (j, 0)),
        ],
        out_specs=pl.BlockSpec((1, tm, d), lambda bi, i, j: (bi, i, 0)),
        out_shape=jax.ShapeDtypeStruct((b, s, d), jnp.float32),
        scratch_shapes=[pltpu.VMEM((tm, d), jnp.bfloat16), pltpu.VMEM((tm, d), jnp.float32)],
        compiler_params=_params("parallel", "parallel", "arbitrary"),
        name="mlp",
    )(x, gain, scale, shift, gate, w1, w2)


def _reorder_w_in(w_in, d_model):
    assert d_model == D_MODEL
    main = 3 * DA_WIDTH + 4 * DN_WIDTH
    small = 4 * DN_HEADS
    head = w_in[:, :main]
    ba = w_in[:, main:main + small]
    gates = w_in[:, main + small:]
    pad = jnp.zeros((w_in.shape[0], LANES - small), w_in.dtype)
    return (jnp.concatenate([gates, head], axis=1).astype(jnp.bfloat16),
            jnp.concatenate([ba, pad], axis=1).astype(jnp.bfloat16))


def _rope_tables(seq):
    pos = jnp.arange(seq, dtype=jnp.float32)
    inv = 1.0 / (ROPE_THETA ** (jnp.arange(0, DA_HEAD_DIM, 2, dtype=jnp.float32) / DA_HEAD_DIM))
    ang = pos[:, None] * inv[None, :]
    ang = jnp.concatenate([ang, ang, ang, ang], axis=-1)
    sign = jnp.where((jnp.arange(LANES) % DA_HEAD_DIM) < DA_HEAD_DIM // 2, -1.0, 1.0)
    return jnp.cos(ang), jnp.sin(ang) * sign


def kernel(x, c, ada_w, ada_b, norm1_g, norm2_g, w_in, dn_conv, dn_A_log, dn_dt_bias, dn_out_gain,
           qk_norm_gain, diff_lambda, diff_subln_gain, w_branch_a, w_branch_b, w_out, w_mlp1, w_mlp2):
    b, s, d = x.shape
    depth = ada_w.shape[0]
    assert s % 512 == 0 and d % LANES == 0 and b <= SUBLANES
    tm = min(1024, s)
    t_gdn = min(512, s)
    tq = min(512, s)
    tk = min(512, s // ATTN_UNROLL)

    c_pad = jnp.zeros((SUBLANES, d), jnp.float32).at[:b].set(c)
    mod = _adaln(c_pad, ada_w, ada_b)[:, :b]
    cos, sin_signed = _rope_tables(s)
    small = 4 * DN_HEADS

    for l in range(depth):
        sh1, sc1, gt1, sh2, sc2, gt2 = [mod[l, :, None, k * d:(k + 1) * d] for k in range(6)]
        lambda_init = 0.8 - 0.6 * math.exp(-0.3 * l)

        w_in_l, w_ba_l = _reorder_w_in(w_in[l], d)
        proj, ba = _inproj(x, norm1_g[l][None], sc1, sh1, w_in_l, w_ba_l, min(512, s))

        qgain = jnp.tile(qk_norm_gain[l, 0], 2)[None]
        kgain = jnp.tile(qk_norm_gain[l, 1], 2)[None]
        qs, kt, v = _attn_prep(proj, cos, sin_signed, qgain, kgain, min(512, s))
        y_a = _attn(qs, kt, v, diff_lambda[l], diff_subln_gain[l][None], lambda_init, tq, tk)

        conv_w = dn_conv[l].reshape(CONV_K, 3, DN_WIDTH).transpose(1, 0, 2)
        row = jnp.zeros((1, LANES), jnp.float32)
        a_log_row = row.at[0, small // 2:small].set(dn_A_log[l].reshape(-1))
        dt_bias_row = row.at[0, small // 2:small].set(dn_dt_bias[l].reshape(-1))
        wq, u, qkkd, gl = _gdn_prep(proj, ba, conv_w, a_log_row, dt_bias_row, t_gdn)
        o_f, o_b = _gdn_scan(wq, u, qkkd, gl, t_gdn)

        x = _merge(o_f, o_b, proj, y_a, x, gt1, dn_out_gain[l][None],
                   w_branch_a[l].astype(jnp.bfloat16), w_branch_b[l].astype(jnp.bfloat16),
                   w_out[l].astype(jnp.bfloat16), min(512, s))
        x = _mlp(x, norm2_g[l][None], sc2, sh2, gt2,
                 w_mlp1[l].astype(jnp.bfloat16), w_mlp2[l].astype(jnp.bfloat16), tm, 1024)
    return x
```

```python
import functools
import math

import jax
import jax.numpy as jnp
from jax import lax
from jax.experimental import pallas as pl
from jax.experimental.pallas import tpu as pltpu

DA_HEADS = 4
DA_HEAD_DIM = 64
DA_WIDTH = DA_HEADS * 2 * DA_HEAD_DIM
DN_HEADS = 4
DN_HEAD_DIM = 128
DN_WIDTH = DN_HEADS * DN_HEAD_DIM
CONV_K = 5
CHUNK = 64
ROPE_THETA = 10000.0
EPS = 1e-6
LOG2_E = math.log2(math.e)

LANES = 128
SUBLANES = 8
VMEM_LIMIT_BYTES = 56 * 1024 * 1024

D_MODEL = 1024
COL_GATE = 0
COL_DA_Q = 2 * D_MODEL // LANES
COL_DA_K = COL_DA_Q + DA_WIDTH // LANES
COL_DA_V = COL_DA_K + DA_WIDTH // LANES
COL_DN_Q = COL_DA_V + DA_WIDTH // LANES
COL_DN_K = COL_DN_Q + DN_WIDTH // LANES
COL_DN_V = COL_DN_K + DN_WIDTH // LANES
COL_DN_Z = COL_DN_V + DN_WIDTH // LANES


def _params(*sem):
    return pltpu.CompilerParams(dimension_semantics=sem, vmem_limit_bytes=VMEM_LIMIT_BYTES)


def _silu(x):
    return x * jax.nn.sigmoid(x)


def _softplus(x):
    return jnp.maximum(x, 0.0) + jnp.log(1.0 + jnp.exp(-jnp.abs(x)))


def _dot(a, b):
    return jnp.dot(a, b, preferred_element_type=jnp.float32)


def _dot_nt(a, b):
    return lax.dot_general(a, b, (((1,), (1,)), ((), ())), preferred_element_type=jnp.float32)


def _split3(x):
    hi = x.astype(jnp.bfloat16).astype(jnp.float32)
    r1 = x - hi
    mid = r1.astype(jnp.bfloat16).astype(jnp.float32)
    lo = (r1 - mid).astype(jnp.bfloat16).astype(jnp.float32)
    return hi, mid, lo


def _adaln_kernel(c_ref, w_ref, b_ref, o_ref):
    c_act = _silu(c_ref[...])
    o_ref[0] = jnp.dot(c_act, w_ref[0], preferred_element_type=jnp.float32,
                       precision=lax.Precision.HIGHEST) + b_ref[0]


def _adaln(c_pad, ada_w, ada_b):
    depth, d, n = ada_w.shape
    tn = 1536
    return pl.pallas_call(
        _adaln_kernel,
        grid=(depth, n // tn),
        in_specs=[
            pl.BlockSpec((SUBLANES, d), lambda l, j: (0, 0)),
            pl.BlockSpec((1, d, tn), lambda l, j: (l, 0, j)),
            pl.BlockSpec((1, 1, tn), lambda l, j: (l, 0, j)),
        ],
        out_specs=pl.BlockSpec((1, SUBLANES, tn), lambda l, j: (l, 0, j)),
        out_shape=jax.ShapeDtypeStruct((depth, SUBLANES, n), jnp.float32),
        compiler_params=_params("parallel", "parallel"),
        name="adaln",
    )(c_pad, ada_w, ada_b.reshape(depth, 1, n))


def _modulated_norm(x, gain, scale, shift):
    y = x * lax.rsqrt(jnp.mean(x * x, axis=-1, keepdims=True) + EPS)
    return (y * gain) * (1.0 + scale) + shift


INPROJ_COLS = 512


def _inproj_kernel(x_ref, g_ref, sc_ref, sh_ref, w_ref, wba_ref, o_ref, ba_ref):
    h = _modulated_norm(x_ref[0], g_ref[...], sc_ref[0], sh_ref[0]).astype(jnp.bfloat16)
    for c in range(0, w_ref.shape[1], INPROJ_COLS):
        o_ref[0, :, c:c + INPROJ_COLS] = _dot(h, w_ref[:, c:c + INPROJ_COLS]).astype(o_ref.dtype)
    ba_ref[0] = _dot(h, wba_ref[...])


def _inproj(x, gain, scale, shift, w, w_ba, tm):
    b, s, d = x.shape
    n = w.shape[1]
    assert n % INPROJ_COLS == 0
    resident = dict(pipeline_mode=pl.Buffered(1))
    return pl.pallas_call(
        _inproj_kernel,
        grid=(b, s // tm),
        in_specs=[
            pl.BlockSpec((1, tm, d), lambda bi, i: (bi, i, 0)),
            pl.BlockSpec((1, d), lambda bi, i: (0, 0)),
            pl.BlockSpec((1, 1, d), lambda bi, i: (bi, 0, 0)),
            pl.BlockSpec((1, 1, d), lambda bi, i: (bi, 0, 0)),
            pl.BlockSpec((d, n), lambda bi, i: (0, 0), **resident),
            pl.BlockSpec((d, LANES), lambda bi, i: (0, 0), **resident),
        ],
        out_specs=[pl.BlockSpec((1, tm, n), lambda bi, i: (bi, i, 0)),
                   pl.BlockSpec((1, tm, LANES), lambda bi, i: (bi, i, 0))],
        out_shape=[jax.ShapeDtypeStruct((b, s, n), jnp.bfloat16),
                   jax.ShapeDtypeStruct((b, s, LANES), jnp.float32)],
        compiler_params=_params("parallel", "parallel"),
        name="inproj",
    )(x, gain, scale, shift, w, w_ba)


def _attn_prep_kernel(q_ref, k_ref, v_ref, cos_ref, sin_ref, qg_ref, kg_ref, qo_ref, kt_ref, vo_ref):
    lane = lax.broadcasted_iota(jnp.int32, (1, LANES), 1)
    left = lane < DA_HEAD_DIM
    first_half = (lane % DA_HEAD_DIM) < (DA_HEAD_DIM // 2)
    cos = cos_ref[...]
    sin = sin_ref[...]

    lane_col = lax.broadcasted_iota(jnp.int32, (LANES, 1), 0)
    same_map = ((lane_col < DA_HEAD_DIM) == left).astype(jnp.bfloat16)

    def norm_rope(x, gain):
        sq = x * x
        hi = sq.astype(jnp.bfloat16)
        lo = (sq - hi.astype(jnp.float32)).astype(jnp.bfloat16)
        ms = (_dot(hi, same_map) + _dot(lo, same_map)) * (1.0 / DA_HEAD_DIM)
        y = x * lax.rsqrt(ms + EPS) * gain
        fwd = pltpu.roll(y, LANES - DA_HEAD_DIM // 2, axis=1)
        bwd = pltpu.roll(y, DA_HEAD_DIM // 2, axis=1)
        return y * cos + jnp.where(first_half, fwd, bwd) * sin

    q = norm_rope(q_ref[0].astype(jnp.float32), qg_ref[...]) * (DA_HEAD_DIM ** -0.5 * LOG2_E)
    k = norm_rope(k_ref[0].astype(jnp.float32), kg_ref[...])
    qo_ref[0, 0, 0] = jnp.where(left, q, 0.0).astype(qo_ref.dtype)
    qo_ref[0, 0, 1] = jnp.where(left, 0.0, q).astype(qo_ref.dtype)
    kt_ref[0, 0] = k.T.astype(kt_ref.dtype)
    v = v_ref[0].astype(vo_ref.dtype)
    vo_ref[0, 0] = jnp.concatenate([v, jnp.ones_like(v)], axis=1)


def _attn_prep(proj, cos, sin_signed, qgain, kgain, ts):
    b, s, _ = proj.shape
    h = DA_HEADS
    col = lambda base: (lambda bi, hi, i: (bi, i, base + hi))
    return pl.pallas_call(
        _attn_prep_kernel,
        grid=(b, h, s // ts),
        in_specs=[
            pl.BlockSpec((1, ts, LANES), col(COL_DA_Q)),
            pl.BlockSpec((1, ts, LANES), col(COL_DA_K)),
            pl.BlockSpec((1, ts, LANES), col(COL_DA_V)),
            pl.BlockSpec((ts, LANES), lambda bi, hi, i: (i, 0)),
            pl.BlockSpec((ts, LANES), lambda bi, hi, i: (i, 0)),
            pl.BlockSpec((1, LANES), lambda bi, hi, i: (0, 0)),
            pl.BlockSpec((1, LANES), lambda bi, hi, i: (0, 0)),
        ],
        out_specs=[
            pl.BlockSpec((1, 1, 2, ts, LANES), lambda bi, hi, i: (bi, hi, 0, i, 0)),
            pl.BlockSpec((1, 1, LANES, ts), lambda bi, hi, i: (bi, hi, 0, i)),
            pl.BlockSpec((1, 1, ts, 2 * LANES), lambda bi, hi, i: (bi, hi, i, 0)),
        ],
        out_shape=[
            jax.ShapeDtypeStruct((b, h, 2, s, LANES), jnp.bfloat16),
            jax.ShapeDtypeStruct((b, h, LANES, s), jnp.bfloat16),
            jax.ShapeDtypeStruct((b, h, s, 2 * LANES), jnp.bfloat16),
        ],
        compiler_params=_params("parallel", "parallel", "parallel"),
        name="attn_prep",
    )(proj, proj, proj, cos, sin_signed, qgain, kgain)


ATTN_UNROLL = 8


ATTN_ROW_BLOCK = 32


def _attn_kernel(q_ref, kt_ref, v_ref, lam_ref, g_ref, o_ref, s_ref, p_ref, m_ref, alpha_ref, acc_ref,
                 *, tq, tk, lambda_init):
    s_len = v_ref.shape[2]
    q = q_ref[0, 0].reshape(2 * tq, LANES)
    n_kv = s_len // tk
    groups = tk // LANES
    rb = ATTN_ROW_BLOCK

    m_ref[...] = jnp.full(m_ref.shape, -jnp.inf, jnp.float32)
    acc_ref[...] = jnp.zeros(acc_ref.shape, jnp.float32)

    def scores(j, slot):
        s_ref[slot] = _dot(q, kt_ref[0, 0, :, pl.ds(pl.multiple_of(j * tk, tk), tk)])

    def update(j, slot):
        for r in range(2 * tq // rb):
            rows = slice(r * rb, (r + 1) * rb)
            sg = [s_ref[slot, rows, g * LANES:(g + 1) * LANES] for g in range(groups)]
            m_old = m_ref[rows, :]
            m_new = jnp.maximum(m_old, jnp.max(functools.reduce(jnp.maximum, sg), axis=-1, keepdims=True))
            alpha_ref[rows, :] = jnp.exp2(m_old - m_new)
            m_ref[rows, :] = m_new
            for g in range(groups):
                p_ref[rows, g * LANES:(g + 1) * LANES] = jnp.exp2(sg[g] - m_new).astype(jnp.bfloat16)
        alpha = alpha_ref[...]
        pv = _dot(p_ref[...], v_ref[0, 0, pl.ds(pl.multiple_of(j * tk, tk), tk), :])
        acc_ref[...] = jnp.concatenate([alpha, alpha], axis=1) * acc_ref[...] + pv

    def body(jj, carry):
        j = ATTN_UNROLL * jj
        for k in range(ATTN_UNROLL):
            scores(jnp.minimum(j + k + 1, n_kv - 1), (k + 1) % 2)
            update(j + k, k % 2)
        return carry

    scores(0, 0)
    lax.fori_loop(0, n_kv // ATTN_UNROLL, body, 0)
    o = acc_ref[:, :LANES] / acc_ref[:, LANES:]
    lq = lam_ref[...]
    lam = (jnp.exp(jnp.sum(lq[0:1] * lq[1:2], axis=-1, keepdims=True))
           - jnp.exp(jnp.sum(lq[2:3] * lq[3:4], axis=-1, keepdims=True)) + lambda_init)
    d = o[:tq] - lam * o[tq:]
    y = d * lax.rsqrt(jnp.mean(d * d, axis=-1, keepdims=True) + EPS) * g_ref[...]
    o_ref[0] = (y * (1.0 - lambda_init)).astype(o_ref.dtype)


def _attn(qs, kt, v, lam_params, subln_gain, lambda_init, tq, tk):
    b, h, _, s, _ = qs.shape
    assert s % (tk * ATTN_UNROLL) == 0 and ATTN_UNROLL % 2 == 0 and (2 * tq) % ATTN_ROW_BLOCK == 0
    return pl.pallas_call(
        functools.partial(_attn_kernel, tq=tq, tk=tk, lambda_init=lambda_init),
        grid=(b, h, s // tq),
        in_specs=[
            pl.BlockSpec((1, 1, 2, tq, LANES), lambda bi, hi, i: (bi, hi, 0, i, 0)),
            pl.BlockSpec((1, 1, LANES, s), lambda bi, hi, i: (bi, hi, 0, 0)),
            pl.BlockSpec((1, 1, s, 2 * LANES), lambda bi, hi, i: (bi, hi, 0, 0)),
            pl.BlockSpec(lam_params.shape, lambda bi, hi, i: (0, 0)),
            pl.BlockSpec((1, LANES), lambda bi, hi, i: (0, 0)),
        ],
        out_specs=pl.BlockSpec((1, tq, LANES), lambda bi, hi, i: (bi, i, hi)),
        out_shape=jax.ShapeDtypeStruct((b, s, h * LANES), jnp.bfloat16),
        scratch_shapes=[pltpu.VMEM((2, 2 * tq, tk), jnp.float32), pltpu.VMEM((2 * tq, tk), jnp.bfloat16),
                        pltpu.VMEM((2 * tq, LANES), jnp.float32), pltpu.VMEM((2 * tq, LANES), jnp.float32),
                        pltpu.VMEM((2 * tq, 2 * LANES), jnp.float32)],
        compiler_params=_params("parallel", "parallel", "parallel"),
        name="attn",
    )(qs, kt, v, lam_params, subln_gain)


INV_BLOCK = 16
HALO = 16


GDN_UNROLL = 8


def _gdn_prep_kernel(q_ref, qp_ref, qn_ref, k_ref, kp_ref, kn_ref, v_ref, vp_ref, vn_ref,
                     cw_ref, ba_ref, alog_ref, dtb_ref,
                     wq_ref, u_ref, qkkd_ref, gl_ref,
                     qs_ref, ks_ref, vs_ref, bf_ref, bb_ref, gf_ref, gb_ref, xq_ref, xk_ref, xv_ref,
                     *, t, nblk):
    i = pl.program_id(1)
    pad = CONV_K // 2
    bf = jnp.bfloat16

    def conv_silu(h, x_ref, p_ref, n_ref, w, xp_ref):
        f32 = jnp.float32
        cols = slice(h * LANES, (h + 1) * LANES)
        xp_ref[h, 0:HALO, :] = jnp.where(i == 0, 0.0, p_ref[0, :, cols].astype(f32))
        xp_ref[h, HALO:HALO + t, :] = x_ref[0, :, cols].astype(f32)
        xp_ref[h, HALO + t:2 * HALO + t, :] = jnp.where(i == nblk - 1, 0.0, n_ref[0, :, cols].astype(f32))
        acc = xp_ref[h, HALO - pad:HALO - pad + t, :] * w[0:1, cols]
        for j in range(1, CONV_K):
            acc = acc + xp_ref[h, HALO - pad + j:HALO - pad + j + t, :] * w[j:j + 1, cols]
        return _silu(acc)

    def l2n(x):
        return x * lax.rsqrt(jnp.sum(x * x, axis=-1, keepdims=True) + EPS)

    cw = cw_ref[...]
    lane = lax.broadcasted_iota(jnp.int32, (1, LANES), 1)

    r3 = lax.broadcasted_iota(jnp.int32, (CHUNK, 3 * CHUNK), 0)
    c3 = lax.broadcasted_iota(jnp.int32, (CHUNK, 3 * CHUNK), 1) % CHUNK
    tril3 = (r3 >= c3).astype(bf)
    triu3 = (r3 <= c3).astype(bf)

    ba = ba_ref[0]
    g_all = -jnp.exp(alog_ref[...]) * _softplus(ba + dtb_ref[...])
    beta_all = jax.nn.sigmoid(ba)
    hi, mid, lo = _split3(g_all)
    gcf, gcb = [], []
    for c in range(t // CHUNK):
        rows = slice(c * CHUNK, (c + 1) * CHUNK)
        parts = jnp.concatenate([hi[rows], mid[rows], lo[rows]], axis=0).astype(bf)
        gcf.append(_dot(tril3, parts))
        gcb.append(_dot(triu3, parts))

    def pick(x, idx):
        col = jnp.sum(jnp.where(lane == idx, x, 0.0), axis=-1, keepdims=True)
        return jnp.broadcast_to(col, (t, LANES))

    gcf = jnp.concatenate(gcf, axis=0)
    gcb = jnp.concatenate(gcb, axis=0)

    row = lax.broadcasted_iota(jnp.int32, (CHUNK, 2 * CHUNK), 0)
    col2 = lax.broadcasted_iota(jnp.int32, (CHUNK, 2 * CHUNK), 1)
    left = col2 < CHUNK
    colm = col2 % CHUNK
    keep2 = (left & (row >= colm)) | (~left & (row <= colm))
    strict2 = keep2 & (row != colm)
    diag2 = row == colm
    eye2 = diag2.astype(jnp.float32)
    same2 = (row // INV_BLOCK) == (colm // INV_BLOCK)
    row4 = lax.broadcasted_iota(jnp.int32, (2 * CHUNK, 2 * CHUNK), 0)
    col4 = lax.broadcasted_iota(jnp.int32, (2 * CHUNK, 2 * CHUNK), 1)
    bd_mask = ((row4 < CHUNK) == (col4 < CHUNK)).astype(bf)

    def blockdiag(y):
        yb = y.astype(bf)
        return jnp.concatenate([yb, yb], axis=0) * bd_mask

    def st_load(h, c):
        rows = pl.ds(pl.multiple_of(c * CHUNK, CHUNK), CHUNK)
        v = dict(h=h, c=c, rows=rows, qc=qs_ref[h, rows, :], kc=ks_ref[h, rows, :], vc=vs_ref[h, rows, :],
                 beta_f=bf_ref[h, rows, :], beta_b=bb_ref[h, rows, :],
                 gc_f=gf_ref[h, rows, :], gc_b=gb_ref[h, rows, :])
        kcb = v['kc'].astype(bf)
        kk = jnp.concatenate([kcb, kcb], axis=0)
        v['gram2'] = _dot_nt(kcb, kk)
        v['qk2'] = _dot_nt(v['qc'].astype(bf), kk)
        return v

    def st_lmat(v):
        gc2 = jnp.where(left, v['gc_f'], v['gc_b'])
        gc_row = jnp.sum(jnp.where(diag2, gc2, 0.0), axis=0, keepdims=True)
        decay2 = jnp.where(keep2, jnp.exp(jnp.where(keep2, gc2 - gc_row, 0.0)), 0.0)
        l2 = jnp.where(strict2, jnp.where(left, v['beta_f'], v['beta_b']) * v['gram2'] * decay2, 0.0)
        ld = jnp.where(same2, l2, 0.0)
        v.update(qkm=(v['qk2'] * decay2).astype(bf), lo=l2 - ld, p=eye2 - ld,
                 m=_dot(ld.astype(bf), blockdiag(ld)))
        del v['gram2'], v['qk2']
        return v

    def st_level(v):
        r = _dot(jnp.concatenate([v['p'], v['m']], axis=0).astype(bf), blockdiag(v['m']))
        v.update(p=v['p'] + r[:CHUNK], m=r[CHUNK:])
        return v

    def st_dinv(v):
        v['d_inv'] = v['p'] + _dot(v['p'].astype(bf), blockdiag(v['m']))
        return v

    def st_n(v):
        v['n'] = _dot(v['d_inv'].astype(bf), blockdiag(v['lo']))
        return v

    def st_nsq(v):
        v['nsq'] = _dot(v['n'].astype(bf), blockdiag(v['n']))
        return v

    def st_pn(v):
        pn = eye2 - v['n']
        v['pn'] = pn + _dot(pn.astype(bf), blockdiag(v['nsq']))
        return v

    def st_t(v):
        v['t2'] = _dot(v['pn'].astype(bf), blockdiag(v['d_inv']))
        return v

    def st_uw(v):
        kc, vc = v['kc'], v['vc']
        v['egc_f'] = jnp.exp(v['gc_f'])
        v['egc_b'] = jnp.exp(v['gc_b'])
        rhs = jnp.concatenate([
            jnp.concatenate([vc * v['beta_f'], kc * (v['beta_f'] * v['egc_f'])], axis=1),
            jnp.concatenate([vc * v['beta_b'], kc * (v['beta_b'] * v['egc_b'])], axis=1)], axis=0).astype(bf)
        v['uw'] = _dot(blockdiag(v['t2']), rhs)
        return v

    def st_store(v):
        h, c, rows, uw, qc, kc = v['h'], v['c'], v['rows'], v['uw'], v['qc'], v['kc']
        r2 = pl.multiple_of(c * 2 * CHUNK, 2 * CHUNK)
        for d, egc in ((0, v['egc_f']), (1, v['egc_b'])):
            u_ref[d, 0, h, rows, :] = uw[d * CHUNK:(d + 1) * CHUNK, :LANES]
            wq_ref[d, 0, h, pl.ds(r2, CHUNK), :] = uw[d * CHUNK:(d + 1) * CHUNK, LANES:].astype(bf)
            wq_ref[d, 0, h, pl.ds(r2 + CHUNK, CHUNK), :] = (qc * egc).astype(bf)
        gtot_f = v['gc_f'][CHUNK - 1:CHUNK, :]
        gtot_b = v['gc_b'][0:1, :]
        kd = jnp.concatenate([kc * jnp.exp(gtot_f - v['gc_f']), kc * jnp.exp(gtot_b - v['gc_b'])], axis=0)
        r3_ = pl.multiple_of(c * 3 * CHUNK, CHUNK)
        qkkd_ref[0, h, pl.ds(r3_, CHUNK), :] = v['qkm']
        qkkd_ref[0, h, pl.ds(r3_ + CHUNK, 2 * CHUNK), :] = kd.T.astype(bf)
        r8 = pl.multiple_of(c * SUBLANES, SUBLANES)
        gl_ref[0, 0, h, pl.ds(r8, SUBLANES), :] = jnp.broadcast_to(jnp.exp(gtot_f), (SUBLANES, LANES))
        gl_ref[1, 0, h, pl.ds(r8, SUBLANES), :] = jnp.broadcast_to(jnp.exp(gtot_b), (SUBLANES, LANES))

    levels = [st_level] * (int(math.log2(INV_BLOCK)) - 2)
    stages = [st_lmat] + levels + [st_dinv, st_n, st_nsq, st_pn, st_t, st_uw]

    for h in range(DN_HEADS):
        qs_ref[h] = l2n(conv_silu(h, q_ref, qp_ref, qn_ref, cw[0], xq_ref)) * (DN_HEAD_DIM ** -0.5)
        ks_ref[h] = l2n(conv_silu(h, k_ref, kp_ref, kn_ref, cw[1], xk_ref))
        vs_ref[h] = conv_silu(h, v_ref, vp_ref, vn_ref, cw[2], xv_ref)
        bf_ref[h] = pick(beta_all, h)
        bb_ref[h] = pick(beta_all, DN_HEADS + h)
        gf_ref[h] = pick(gcf, 2 * DN_HEADS + h)
        gb_ref[h] = pick(gcb, 3 * DN_HEADS + h)

        def trip(j, carry, h=h):
            vals = [st_load(h, j * GDN_UNROLL + k) for k in range(GDN_UNROLL)]
            for stage in stages:
                vals = [stage(v) for v in vals]
            for v in vals:
                st_store(v)
            return carry

        lax.fori_loop(0, t // (CHUNK * GDN_UNROLL), trip, 0)


def _gdn_prep(proj, ba, conv_w, a_log_row, dt_bias_row, t):
    b, s, _ = proj.shape
    nh = DN_HEADS
    nblk = s // t
    n = s // CHUNK
    tpb = t // HALO

    wcol = DN_WIDTH // LANES

    def main(base):
        return pl.BlockSpec((1, t, DN_WIDTH), lambda bi, i: (bi, i, base // wcol))

    def prev(base):
        return pl.BlockSpec((1, HALO, DN_WIDTH),
                            lambda bi, i: (bi, jnp.maximum(i * tpb - 1, 0), base // wcol))

    def nxt(base):
        return pl.BlockSpec((1, HALO, DN_WIDTH),
                            lambda bi, i: (bi, jnp.minimum((i + 1) * tpb, s // HALO - 1), base // wcol))

    in_specs = []
    for base in (COL_DN_Q, COL_DN_K, COL_DN_V):
        in_specs += [main(base), prev(base), nxt(base)]
    in_specs += [
        pl.BlockSpec((3, CONV_K, DN_WIDTH), lambda bi, i: (0, 0, 0)),
        pl.BlockSpec((1, t, LANES), lambda bi, i: (bi, i, 0)),
        pl.BlockSpec((1, LANES), lambda bi, i: (0, 0)),
        pl.BlockSpec((1, LANES), lambda bi, i: (0, 0)),
    ]
    out_specs = [
        pl.BlockSpec((2, 1, nh, 2 * t, LANES), lambda bi, i: (0, bi, 0, i, 0)),
        pl.BlockSpec((2, 1, nh, t, LANES), lambda bi, i: (0, bi, 0, i, 0)),
        pl.BlockSpec((1, nh, 3 * t, LANES), lambda bi, i: (bi, 0, i, 0)),
        pl.BlockSpec((2, 1, nh, t // CHUNK * SUBLANES, LANES), lambda bi, i: (0, bi, 0, i, 0)),
    ]
    out_shape = [
        jax.ShapeDtypeStruct((2, b, nh, 2 * s, LANES), jnp.bfloat16),
        jax.ShapeDtypeStruct((2, b, nh, s, LANES), jnp.float32),
        jax.ShapeDtypeStruct((b, nh, 3 * s, LANES), jnp.bfloat16),
        jax.ShapeDtypeStruct((2, b, nh, n * SUBLANES, LANES), jnp.float32),
    ]
    proj9 = [proj] * 9
    return pl.pallas_call(
        functools.partial(_gdn_prep_kernel, t=t, nblk=nblk),
        grid=(b, nblk),
        in_specs=in_specs,
        out_specs=out_specs,
        out_shape=out_shape,
        scratch_shapes=([pltpu.VMEM((nh, t, LANES), jnp.float32)] * 7
                        + [pltpu.VMEM((nh, t + 2 * HALO, LANES), jnp.float32)] * 3),
        compiler_params=_params("parallel", "parallel"),
        name="gdn_prep",
    )(*proj9, conv_w, ba, a_log_row, dt_bias_row)


def _gdn_scan_kernel(wq_f, u_f, qkkd_f, gl_f, wq_b, u_b, qkkd_b, gl_b, o_f, o_b, state_ref, *, t):
    nc = t // CHUNK
    nb = u_f.shape[1]

    @pl.when(pl.program_id(0) == 0)
    def _():
        state_ref[...] = jnp.zeros_like(state_ref)

    streams = ((0, wq_f, u_f, qkkd_f, gl_f, o_f), (1, wq_b, u_b, qkkd_b, gl_b, o_b))
    zeros = jnp.zeros((CHUNK, DN_HEAD_DIM), jnp.bfloat16)

    def body(c, carry):
        work = []
        for d, wq, u, qkkd, gl, o in streams:
            cc = c if d == 0 else nc - 1 - c
            r2 = pl.multiple_of(cc * 2 * CHUNK, 2 * CHUNK)
            for bi in range(nb):
                for hh in range(DN_HEADS):
                    st = state_ref[d, bi, hh]
                    ws = _dot(wq[0, bi, hh, pl.ds(r2, 2 * CHUNK), :], st.astype(jnp.bfloat16))
                    work.append((d, bi, hh, cc, u, qkkd, gl, o, st, ws))
        for d, bi, hh, cc, u, qkkd, gl, o, st, ws in work:
            r0 = pl.multiple_of(cc * CHUNK, CHUNK)
            r3 = pl.multiple_of(cc * 3 * CHUNK, CHUNK)
            r8 = pl.multiple_of(cc * SUBLANES, SUBLANES)
            v_new = (u[0, bi, hh, pl.ds(r0, CHUNK), :] - ws[:CHUNK]).astype(jnp.bfloat16)
            rhs = jnp.concatenate([v_new, zeros] if d == 0 else [zeros, v_new], axis=0)
            r = _dot(qkkd[bi, hh, pl.ds(r3, 3 * CHUNK), :], rhs)
            o[bi, hh, pl.ds(r0, CHUNK), :] = ws[CHUNK:] + r[:CHUNK]
            decay = gl[0, bi, hh, pl.ds(r8, SUBLANES), :][0:1]
            state_ref[d, bi, hh] = st * decay + r[CHUNK:]
        return carry

    lax.fori_loop(0, nc, body, 0)


def _gdn_scan(wq, u, qkkd, gl, t):
    _, b, nh, s, _ = u.shape
    nblk = s // t
    g8 = t // CHUNK * SUBLANES

    def specs(d):
        blk = (lambda i: i) if d == 0 else (lambda i: nblk - 1 - i)
        idx = lambda i: (d, 0, 0, blk(i), 0)
        idx4 = lambda i: (0, 0, blk(i), 0)
        return [
            pl.BlockSpec((1, b, nh, 2 * t, LANES), idx),
            pl.BlockSpec((1, b, nh, t, LANES), idx),
            pl.BlockSpec((b, nh, 3 * t, LANES), idx4),
            pl.BlockSpec((1, b, nh, g8, LANES), idx),
        ], pl.BlockSpec((b, nh, t, LANES), idx4)

    in_f, out_f = specs(0)
    in_b, out_b = specs(1)
    o_shape = jax.ShapeDtypeStruct((b, nh, s, LANES), jnp.float32)
    return pl.pallas_call(
        functools.partial(_gdn_scan_kernel, t=t),
        grid=(nblk,),
        in_specs=in_f + in_b,
        out_specs=[out_f, out_b],
        out_shape=[o_shape, o_shape],
        scratch_shapes=[pltpu.VMEM((2, b, nh, DN_HEAD_DIM, DN_HEAD_DIM), jnp.float32)],
        compiler_params=_params("arbitrary"),
        name="gdn_scan",
    )(wq, u, qkkd, gl, wq, u, qkkd, gl)


def _merge_kernel(of_ref, ob_ref, z_ref, ga_ref, gb_ref, ya_ref, x_ref, gt_ref, og_ref,
                  wa_ref, wb_ref, wo_ref, o_ref):
    og = og_ref[...]
    z = z_ref[0].astype(jnp.float32)
    parts = []
    for hh in range(DN_HEADS):
        o = of_ref[0, hh] + ob_ref[0, hh]
        y = o * lax.rsqrt(jnp.mean(o * o, axis=-1, keepdims=True) + EPS) * og
        parts.append((y * _silu(z[:, hh * LANES:(hh + 1) * LANES])).astype(jnp.bfloat16))
    yb = jnp.concatenate(parts, axis=1)
    pa = _dot(ya_ref[0], wa_ref[...])
    pb = _dot(yb, wb_ref[...])
    merged = (jax.nn.sigmoid(ga_ref[0].astype(jnp.float32)) * pa
              + jax.nn.sigmoid(gb_ref[0].astype(jnp.float32)) * pb)
    o_ref[0] = x_ref[0] + gt_ref[0] * _dot(merged.astype(jnp.bfloat16), wo_ref[...])


def _merge(o_f, o_b, proj, y_a, x, gate, out_gain, w_a, w_b, w_o, tm):
    b, s, d = x.shape
    nh = DN_HEADS
    zw = DN_WIDTH // LANES
    return pl.pallas_call(
        _merge_kernel,
        grid=(b, s // tm),
        in_specs=[
            pl.BlockSpec((1, nh, tm, LANES), lambda bi, i: (bi, 0, i, 0)),
            pl.BlockSpec((1, nh, tm, LANES), lambda bi, i: (bi, 0, i, 0)),
            pl.BlockSpec((1, tm, DN_WIDTH), lambda bi, i: (bi, i, COL_DN_Z // zw)),
            pl.BlockSpec((1, tm, d), lambda bi, i: (bi, i, 0)),
            pl.BlockSpec((1, tm, d), lambda bi, i: (bi, i, 1)),
            pl.BlockSpec((1, tm, DA_WIDTH), lambda bi, i: (bi, i, 0)),
            pl.BlockSpec((1, tm, d), lambda bi, i: (bi, i, 0)),
            pl.BlockSpec((1, 1, d), lambda bi, i: (bi, 0, 0)),
            pl.BlockSpec((1, LANES), lambda bi, i: (0, 0)),
            pl.BlockSpec(w_a.shape, lambda bi, i: (0, 0)),
            pl.BlockSpec(w_b.shape, lambda bi, i: (0, 0)),
            pl.BlockSpec(w_o.shape, lambda bi, i: (0, 0)),
        ],
        out_specs=pl.BlockSpec((1, tm, d), lambda bi, i: (bi, i, 0)),
        out_shape=jax.ShapeDtypeStruct((b, s, d), jnp.float32),
        compiler_params=_params("parallel", "parallel"),
        name="merge",
    )(o_f, o_b, proj, proj, proj, y_a, x, gate, out_gain, w_a, w_b, w_o)


def _mlp_kernel(x_ref, g_ref, sc_ref, sh_ref, gt_ref, w1_ref, w2_ref, o_ref, h_ref, acc_ref):
    j = pl.program_id(2)

    @pl.when(j == 0)
    def _():
        h_ref[...] = _modulated_norm(x_ref[0], g_ref[...], sc_ref[0], sh_ref[0]).astype(h_ref.dtype)
        acc_ref[...] = jnp.zeros_like(acc_ref)

    a = jnp.maximum(_dot(h_ref[...], w1_ref[...]), 0.0)
    acc_ref[...] += _dot((a * a).astype(jnp.bfloat16), w2_ref[...])

    @pl.when(j == pl.num_programs(2) - 1)
    def _():
        o_ref[0] = x_ref[0] + gt_ref[0] * acc_ref[...]


def _mlp(x, gain, scale, shift, gate, w1, w2, tm, tf):
    b, s, d = x.shape
    f = w1.shape[1]
    vec = pl.BlockSpec((1, 1, d), lambda bi, i, j: (bi, 0, 0))
    return pl.pallas_call(
        _mlp_kernel,
        grid=(b, s // tm, f // tf),
        in_specs=[
            pl.BlockSpec((1, tm, d), lambda bi, i, j: (bi, i, 0)),
            pl.BlockSpec((1, d), lambda bi, i, j: (0, 0)),
            vec, vec, vec,
            pl.BlockSpec((d, tf), lambda bi, i, j: (0, j)),
            pl.BlockSpec((tf, d), lambda bi, i, j: (j, 0)),
        ],
        out_specs=pl.BlockSpec((1, tm, d), lambda bi, i, j: (bi, i, 0)),
        out_shape=jax.ShapeDtypeStruct((b, s, d), jnp.float32),
        scratch_shapes=[pltpu.VMEM((tm, d), jnp.bfloat16), pltpu.VMEM((tm, d), jnp.float32)],
        compiler_params=_params("parallel", "parallel", "arbitrary"),
        name="mlp",
    )(x, gain, scale, shift, gate, w1, w2)


def _reorder_w_in(w_in, d_model):
    assert d_model == D_MODEL
    main = 3 * DA_WIDTH + 4 * DN_WIDTH
    small = 4 * DN_HEADS
    head = w_in[:, :main]
    ba = w_in[:, main:main + small]
    gates = w_in[:, main + small:]
    pad = jnp.zeros((w_in.shape[0], LANES - small), w_in.dtype)
    return (jnp.concatenate([gates, head], axis=1).astype(jnp.bfloat16),
            jnp.concatenate([ba, pad], axis=1).astype(jnp.bfloat16))


def _rope_tables(seq):
    pos = jnp.arange(seq, dtype=jnp.float32)
    inv = 1.0 / (ROPE_THETA ** (jnp.arange(0, DA_HEAD_DIM, 2, dtype=jnp.float32) / DA_HEAD_DIM))
    ang = pos[:, None] * inv[None, :]
    ang = jnp.concatenate([ang, ang, ang, ang], axis=-1)
    sign = jnp.where((jnp.arange(LANES) % DA_HEAD_DIM) < DA_HEAD_DIM // 2, -1.0, 1.0)
    return jnp.cos(ang), jnp.sin(ang) * sign


def kernel(x, c, ada_w, ada_b, norm1_g, norm2_g, w_in, dn_conv, dn_A_log, dn_dt_bias, dn_out_gain,
           qk_norm_gain, diff_lambda, diff_subln_gain, w_branch_a, w_branch_b, w_out, w_mlp1, w_mlp2):
    b, s, d = x.shape
    depth = ada_w.shape[0]
    assert s % 512 == 0 and d % LANES == 0 and b <= SUBLANES
    tm = min(1024, s)
    t_gdn = min(512, s)
    tq = min(512, s)
    tk = min(512, s // ATTN_UNROLL)

    c_pad = jnp.zeros((SUBLANES, d), jnp.float32).at[:b].set(c)
    mod = _adaln(c_pad, ada_w, ada_b)[:, :b]
    cos, sin_signed = _rope_tables(s)
    small = 4 * DN_HEADS

    for l in range(depth):
        sh1, sc1, gt1, sh2, sc2, gt2 = [mod[l, :, None, k * d:(k + 1) * d] for k in range(6)]
        lambda_init = 0.8 - 0.6 * math.exp(-0.3 * l)

        w_in_l, w_ba_l = _reorder_w_in(w_in[l], d)
        proj, ba = _inproj(x, norm1_g[l][None], sc1, sh1, w_in_l, w_ba_l, min(512, s))

        qgain = jnp.tile(qk_norm_gain[l, 0], 2)[None]
        kgain = jnp.tile(qk_norm_gain[l, 1], 2)[None]
        qs, kt, v = _attn_prep(proj, cos, sin_signed, qgain, kgain, min(1024, s))
        y_a = _attn(qs, kt, v, diff_lambda[l], diff_subln_gain[l][None], lambda_init, tq, tk)

        conv_w = dn_conv[l].reshape(CONV_K, 3, DN_WIDTH).transpose(1, 0, 2)
        row = jnp.zeros((1, LANES), jnp.float32)
        a_log_row = row.at[0, small // 2:small].set(dn_A_log[l].reshape(-1))
        dt_bias_row = row.at[0, small // 2:small].set(dn_dt_bias[l].reshape(-1))
        wq, u, qkkd, gl = _gdn_prep(proj, ba, conv_w, a_log_row, dt_bias_row, t_gdn)
        o_f, o_b = _gdn_scan(wq, u, qkkd, gl, t_gdn)

        x = _merge(o_f, o_b, proj, y_a, x, gt1, dn_out_gain[l][None],
                   w_branch_a[l].astype(jnp.bfloat16), w_branch_b[l].astype(jnp.bfloat16),
                   w_out[l].astype(jnp.bfloat16), min(512, s))
        x = _mlp(x, norm2_g[l][None], sc2, sh2, gt2,
                 w_mlp1[l].astype(jnp.bfloat16), w_mlp2[l].astype(jnp.bfloat16), tm, 1024)
    return x
```

```python
import functools
import math

import jax
import jax.numpy as jnp
from jax import lax
from jax.experimental import pallas as pl
from jax.experimental.pallas import tpu as pltpu

DA_HEADS = 4
DA_HEAD_DIM = 64
DA_WIDTH = DA_HEADS * 2 * DA_HEAD_DIM
DN_HEADS = 4
DN_HEAD_DIM = 128
DN_WIDTH = DN_HEADS * DN_HEAD_DIM
CONV_K = 5
CHUNK = 64
ROPE_THETA = 10000.0
EPS = 1e-6
LOG2_E = math.log2(math.e)

LANES = 128
SUBLANES = 8
VMEM_LIMIT_BYTES = 56 * 1024 * 1024

D_MODEL = 1024
COL_GATE = 0
COL_DA_Q = 2 * D_MODEL // LANES
COL_DA_K = COL_DA_Q + DA_WIDTH // LANES
COL_DA_V = COL_DA_K + DA_WIDTH // LANES
COL_DN_Q = COL_DA_V + DA_WIDTH // LANES
COL_DN_K = COL_DN_Q + DN_WIDTH // LANES
COL_DN_V = COL_DN_K + DN_WIDTH // LANES
COL_DN_Z = COL_DN_V + DN_WIDTH // LANES


def _params(*sem):
    return pltpu.CompilerParams(dimension_semantics=sem, vmem_limit_bytes=VMEM_LIMIT_BYTES)


def _silu(x):
    return x * jax.nn.sigmoid(x)


def _softplus(x):
    return jnp.maximum(x, 0.0) + jnp.log(1.0 + jnp.exp(-jnp.abs(x)))


def _dot(a, b):
    return jnp.dot(a, b, preferred_element_type=jnp.float32)


def _dot_nt(a, b):
    return lax.dot_general(a, b, (((1,), (1,)), ((), ())), preferred_element_type=jnp.float32)


def _split3(x):
    hi = x.astype(jnp.bfloat16).astype(jnp.float32)
    r1 = x - hi
    mid = r1.astype(jnp.bfloat16).astype(jnp.float32)
    lo = (r1 - mid).astype(jnp.bfloat16).astype(jnp.float32)
    return hi, mid, lo


def _adaln_kernel(c_ref, w_ref, b_ref, o_ref):
    c_act = _silu(c_ref[...])
    o_ref[0] = jnp.dot(c_act, w_ref[0], preferred_element_type=jnp.float32,
                       precision=lax.Precision.HIGHEST) + b_ref[0]


def _adaln(c_pad, ada_w, ada_b):
    depth, d, n = ada_w.shape
    tn = 1536
    return pl.pallas_call(
        _adaln_kernel,
        grid=(depth, n // tn),
        in_specs=[
            pl.BlockSpec((SUBLANES, d), lambda l, j: (0, 0)),
            pl.BlockSpec((1, d, tn), lambda l, j: (l, 0, j)),
            pl.BlockSpec((1, 1, tn), lambda l, j: (l, 0, j)),
        ],
        out_specs=pl.BlockSpec((1, SUBLANES, tn), lambda l, j: (l, 0, j)),
        out_shape=jax.ShapeDtypeStruct((depth, SUBLANES, n), jnp.float32),
        compiler_params=_params("parallel", "parallel"),
        name="adaln",
    )(c_pad, ada_w, ada_b.reshape(depth, 1, n))


def _modulated_norm(x, gain, scale, shift):
    y = x * lax.rsqrt(jnp.mean(x * x, axis=-1, keepdims=True) + EPS)
    return (y * gain) * (1.0 + scale) + shift


INPROJ_COLS = 512


def _inproj_kernel(x_ref, g_ref, sc_ref, sh_ref, w_ref, wba_ref, o_ref, ba_ref):
    h = _modulated_norm(x_ref[0], g_ref[...], sc_ref[0], sh_ref[0]).astype(jnp.bfloat16)
    for c in range(0, w_ref.shape[1], INPROJ_COLS):
        o_ref[0, :, c:c + INPROJ_COLS] = _dot(h, w_ref[:, c:c + INPROJ_COLS]).astype(o_ref.dtype)
    ba_ref[0] = _dot(h, wba_ref[...])


def _inproj(x, gain, scale, shift, w, w_ba, tm):
    b, s, d = x.shape
    n = w.shape[1]
    assert n % INPROJ_COLS == 0
    resident = dict(pipeline_mode=pl.Buffered(1))
    return pl.pallas_call(
        _inproj_kernel,
        grid=(b, s // tm),
        in_specs=[
            pl.BlockSpec((1, tm, d), lambda bi, i: (bi, i, 0)),
            pl.BlockSpec((1, d), lambda bi, i: (0, 0)),
            pl.BlockSpec((1, 1, d), lambda bi, i: (bi, 0, 0)),
            pl.BlockSpec((1, 1, d), lambda bi, i: (bi, 0, 0)),
            pl.BlockSpec((d, n), lambda bi, i: (0, 0), **resident),
            pl.BlockSpec((d, LANES), lambda bi, i: (0, 0), **resident),
        ],
        out_specs=[pl.BlockSpec((1, tm, n), lambda bi, i: (bi, i, 0)),
                   pl.BlockSpec((1, tm, LANES), lambda bi, i: (bi, i, 0))],
        out_shape=[jax.ShapeDtypeStruct((b, s, n), jnp.bfloat16),
                   jax.ShapeDtypeStruct((b, s, LANES), jnp.float32)],
        compiler_params=_params("parallel", "parallel"),
        name="inproj",
    )(x, gain, scale, shift, w, w_ba)


ONES_ROWS = 16


def _attn_prep_kernel(q_ref, k_ref, v_ref, cos_ref, sin_ref, qg_ref, kg_ref, qo_ref, ko_ref, vo_ref):
    lane = lax.broadcasted_iota(jnp.int32, (1, LANES), 1)
    left = lane < DA_HEAD_DIM
    first_half = (lane % DA_HEAD_DIM) < (DA_HEAD_DIM // 2)
    cos = cos_ref[...]
    sin = sin_ref[...]

    lane_col = lax.broadcasted_iota(jnp.int32, (LANES, 1), 0)
    same_map = ((lane_col < DA_HEAD_DIM) == left).astype(jnp.bfloat16)

    def norm_rope(x, gain):
        sq = x * x
        hi = sq.astype(jnp.bfloat16)
        lo = (sq - hi.astype(jnp.float32)).astype(jnp.bfloat16)
        ms = (_dot(hi, same_map) + _dot(lo, same_map)) * (1.0 / DA_HEAD_DIM)
        y = x * lax.rsqrt(ms + EPS) * gain
        fwd = pltpu.roll(y, LANES - DA_HEAD_DIM // 2, axis=1)
        bwd = pltpu.roll(y, DA_HEAD_DIM // 2, axis=1)
        return y * cos + jnp.where(first_half, fwd, bwd) * sin

    q = norm_rope(q_ref[0].astype(jnp.float32), qg_ref[...]) * (DA_HEAD_DIM ** -0.5 * LOG2_E)
    k = norm_rope(k_ref[0].astype(jnp.float32), kg_ref[...])
    qt = q.T
    upper = lax.broadcasted_iota(jnp.int32, (LANES, 1), 0) < DA_HEAD_DIM
    qo_ref[0, 0, 0] = jnp.where(upper, qt, 0.0).astype(qo_ref.dtype)
    qo_ref[0, 0, 1] = jnp.where(upper, 0.0, qt).astype(qo_ref.dtype)
    ko_ref[0, 0] = k.astype(ko_ref.dtype)
    vt = v_ref[0].astype(jnp.float32).T
    vo_ref[0, 0] = jnp.concatenate([vt, jnp.ones((ONES_ROWS, vt.shape[1]), jnp.float32)],
                                   axis=0).astype(vo_ref.dtype)


def _attn_prep(proj, cos, sin_signed, qgain, kgain, ts):
    b, s, _ = proj.shape
    h = DA_HEADS
    col = lambda base: (lambda bi, hi, i: (bi, i, base + hi))
    return pl.pallas_call(
        _attn_prep_kernel,
        grid=(b, h, s // ts),
        in_specs=[
            pl.BlockSpec((1, ts, LANES), col(COL_DA_Q)),
            pl.BlockSpec((1, ts, LANES), col(COL_DA_K)),
            pl.BlockSpec((1, ts, LANES), col(COL_DA_V)),
            pl.BlockSpec((ts, LANES), lambda bi, hi, i: (i, 0)),
            pl.BlockSpec((ts, LANES), lambda bi, hi, i: (i, 0)),
            pl.BlockSpec((1, LANES), lambda bi, hi, i: (0, 0)),
            pl.BlockSpec((1, LANES), lambda bi, hi, i: (0, 0)),
        ],
        out_specs=[
            pl.BlockSpec((1, 1, 2, LANES, ts), lambda bi, hi, i: (bi, hi, 0, 0, i)),
            pl.BlockSpec((1, 1, ts, LANES), lambda bi, hi, i: (bi, hi, i, 0)),
            pl.BlockSpec((1, 1, LANES + ONES_ROWS, ts), lambda bi, hi, i: (bi, hi, 0, i)),
        ],
        out_shape=[
            jax.ShapeDtypeStruct((b, h, 2, LANES, s), jnp.bfloat16),
            jax.ShapeDtypeStruct((b, h, s, LANES), jnp.bfloat16),
            jax.ShapeDtypeStruct((b, h, LANES + ONES_ROWS, s), jnp.bfloat16),
        ],
        compiler_params=_params("parallel", "parallel", "parallel"),
        name="attn_prep",
    )(proj, proj, proj, cos, sin_signed, qgain, kgain)


ATTN_UNROLL = 16


ATTN_KEY_BLOCK = 64


def _attn_kernel(qt_ref, k_ref, vt_ref, lam_ref, g_ref, o_ref, s_ref, p_ref, m_ref, alpha_ref, acc_ref,
                 *, tq, tk, lambda_init):
    s_len = k_ref.shape[2]
    qt = jnp.concatenate([qt_ref[0, 0, 0], qt_ref[0, 0, 1]], axis=1)
    n_kv = s_len // tk
    col_groups = 2 * tq // LANES
    key_blocks = tk // ATTN_KEY_BLOCK

    m_ref[...] = jnp.full(m_ref.shape, -jnp.inf, jnp.float32)
    acc_ref[...] = jnp.zeros(acc_ref.shape, jnp.float32)

    def scores(j, slot):
        s_ref[slot] = _dot(k_ref[0, 0, pl.ds(pl.multiple_of(j * tk, tk), tk), :], qt)

    def update(j, slot):
        for g in range(col_groups):
            cols = slice(g * LANES, (g + 1) * LANES)

            def tiles(r):
                rows = slice(r * ATTN_KEY_BLOCK, (r + 1) * ATTN_KEY_BLOCK)
                return rows, s_ref[slot, rows, cols].reshape(ATTN_KEY_BLOCK // SUBLANES, SUBLANES, LANES)

            mx = functools.reduce(jnp.maximum, [jnp.max(tiles(r)[1], axis=0) for r in range(key_blocks)])
            m_old = m_ref[:, cols]
            m_new = jnp.maximum(m_old, jnp.max(mx, axis=0, keepdims=True))
            alpha_ref[:, cols] = jnp.exp2(m_old - m_new)
            m_ref[:, cols] = m_new
            for r in range(key_blocks):
                rows, s3 = tiles(r)
                p_ref[rows, cols] = jnp.exp2(s3 - m_new[None]).reshape(ATTN_KEY_BLOCK, LANES).astype(jnp.bfloat16)
        pv = _dot(vt_ref[0, 0, :, pl.ds(pl.multiple_of(j * tk, tk), tk)], p_ref[...])
        acc_ref[...] = alpha_ref[0:1, :] * acc_ref[...] + pv

    def body(jj, carry):
        j = ATTN_UNROLL * jj
        for k in range(ATTN_UNROLL):
            scores(jnp.minimum(j + k + 1, n_kv - 1), (k + 1) % 2)
            update(j + k, k % 2)
        return carry

    scores(0, 0)
    lax.fori_loop(0, n_kv // ATTN_UNROLL, body, 0)
    ot = acc_ref[:LANES, :] / acc_ref[LANES:LANES + 1, :]
    lq = lam_ref[...]
    lam = (jnp.exp(jnp.sum(lq[0:1] * lq[1:2], axis=-1, keepdims=True))
           - jnp.exp(jnp.sum(lq[2:3] * lq[3:4], axis=-1, keepdims=True)) + lambda_init)
    d = (ot[:, :tq] - lam * ot[:, tq:]).T
    y = d * lax.rsqrt(jnp.mean(d * d, axis=-1, keepdims=True) + EPS) * g_ref[...]
    o_ref[0] = (y * (1.0 - lambda_init)).astype(o_ref.dtype)


def _attn(qt, k, vt, lam_params, subln_gain, lambda_init, tq, tk):
    b, h, s, _ = k.shape
    assert s % (tk * ATTN_UNROLL) == 0 and ATTN_UNROLL % 2 == 0 and tk % ATTN_KEY_BLOCK == 0
    return pl.pallas_call(
        functools.partial(_attn_kernel, tq=tq, tk=tk, lambda_init=lambda_init),
        grid=(b, h, s // tq),
        in_specs=[
            pl.BlockSpec((1, 1, 2, LANES, tq), lambda bi, hi, i: (bi, hi, 0, 0, i)),
            pl.BlockSpec((1, 1, s, LANES), lambda bi, hi, i: (bi, hi, 0, 0)),
            pl.BlockSpec((1, 1, LANES + ONES_ROWS, s), lambda bi, hi, i: (bi, hi, 0, 0)),
            pl.BlockSpec(lam_params.shape, lambda bi, hi, i: (0, 0)),
            pl.BlockSpec((1, LANES), lambda bi, hi, i: (0, 0)),
        ],
        out_specs=pl.BlockSpec((1, tq, LANES), lambda bi, hi, i: (bi, i, hi)),
        out_shape=jax.ShapeDtypeStruct((b, s, h * LANES), jnp.bfloat16),
        scratch_shapes=[pltpu.VMEM((2, tk, 2 * tq), jnp.float32), pltpu.VMEM((tk, 2 * tq), jnp.bfloat16),
                        pltpu.VMEM((SUBLANES, 2 * tq), jnp.float32), pltpu.VMEM((SUBLANES, 2 * tq), jnp.float32),
                        pltpu.VMEM((LANES + ONES_ROWS, 2 * tq), jnp.float32)],
        compiler_params=_params("parallel", "parallel", "parallel"),
        name="attn",
    )(qt, k, vt, lam_params, subln_gain)


INV_BLOCK = 16
HALO = 16


GDN_UNROLL = 8


def _gdn_prep_kernel(q_ref, qp_ref, qn_ref, k_ref, kp_ref, kn_ref, v_ref, vp_ref, vn_ref,
                     cw_ref, ba_ref, alog_ref, dtb_ref,
                     wq_ref, u_ref, qkkd_ref, gl_ref,
                     qs_ref, ks_ref, vs_ref, bf_ref, bb_ref, gf_ref, gb_ref, xq_ref, xk_ref, xv_ref,
                     *, t, nblk):
    i = pl.program_id(1)
    pad = CONV_K // 2
    bf = jnp.bfloat16

    def conv_silu(h, x_ref, p_ref, n_ref, w, xp_ref):
        f32 = jnp.float32
        cols = slice(h * LANES, (h + 1) * LANES)
        xp_ref[h, 0:HALO, :] = jnp.where(i == 0, 0.0, p_ref[0, :, cols].astype(f32))
        xp_ref[h, HALO:HALO + t, :] = x_ref[0, :, cols].astype(f32)
        xp_ref[h, HALO + t:2 * HALO + t, :] = jnp.where(i == nblk - 1, 0.0, n_ref[0, :, cols].astype(f32))
        acc = xp_ref[h, HALO - pad:HALO - pad + t, :] * w[0:1, cols]
        for j in range(1, CONV_K):
            acc = acc + xp_ref[h, HALO - pad + j:HALO - pad + j + t, :] * w[j:j + 1, cols]
        return _silu(acc)

    def l2n(x):
        return x * lax.rsqrt(jnp.sum(x * x, axis=-1, keepdims=True) + EPS)

    cw = cw_ref[...]
    lane = lax.broadcasted_iota(jnp.int32, (1, LANES), 1)

    r3 = lax.broadcasted_iota(jnp.int32, (CHUNK, 3 * CHUNK), 0)
    c3 = lax.broadcasted_iota(jnp.int32, (CHUNK, 3 * CHUNK), 1) % CHUNK
    tril3 = (r3 >= c3).astype(bf)
    triu3 = (r3 <= c3).astype(bf)

    ba = ba_ref[0]
    g_all = -jnp.exp(alog_ref[...]) * _softplus(ba + dtb_ref[...])
    beta_all = jax.nn.sigmoid(ba)
    hi, mid, lo = _split3(g_all)
    gcf, gcb = [], []
    for c in range(t // CHUNK):
        rows = slice(c * CHUNK, (c + 1) * CHUNK)
        parts = jnp.concatenate([hi[rows], mid[rows], lo[rows]], axis=0).astype(bf)
        gcf.append(_dot(tril3, parts))
        gcb.append(_dot(triu3, parts))

    def pick(x, idx):
        col = jnp.sum(jnp.where(lane == idx, x, 0.0), axis=-1, keepdims=True)
        return jnp.broadcast_to(col, (t, LANES))

    gcf = jnp.concatenate(gcf, axis=0)
    gcb = jnp.concatenate(gcb, axis=0)

    row = lax.broadcasted_iota(jnp.int32, (CHUNK, 2 * CHUNK), 0)
    col2 = lax.broadcasted_iota(jnp.int32, (CHUNK, 2 * CHUNK), 1)
    left = col2 < CHUNK
    colm = col2 % CHUNK
    keep2 = (left & (row >= colm)) | (~left & (row <= colm))
    strict2 = keep2 & (row != colm)
    diag2 = row == colm
    eye2 = diag2.astype(jnp.float32)
    same2 = (row // INV_BLOCK) == (colm // INV_BLOCK)
    row4 = lax.broadcasted_iota(jnp.int32, (2 * CHUNK, 2 * CHUNK), 0)
    col4 = lax.broadcasted_iota(jnp.int32, (2 * CHUNK, 2 * CHUNK), 1)
    bd_mask = ((row4 < CHUNK) == (col4 < CHUNK)).astype(bf)

    def blockdiag(y):
        yb = y.astype(bf)
        return jnp.concatenate([yb, yb], axis=0) * bd_mask

    def st_load(h, c):
        rows = pl.ds(pl.multiple_of(c * CHUNK, CHUNK), CHUNK)
        v = dict(h=h, c=c, rows=rows, qc=qs_ref[h, rows, :], kc=ks_ref[h, rows, :], vc=vs_ref[h, rows, :],
                 beta_f=bf_ref[h, rows, :], beta_b=bb_ref[h, rows, :],
                 gc_f=gf_ref[h, rows, :], gc_b=gb_ref[h, rows, :])
        kcb = v['kc'].astype(bf)
        kk = jnp.concatenate([kcb, kcb], axis=0)
        v['gram2'] = _dot_nt(kcb, kk)
        v['qk2'] = _dot_nt(v['qc'].astype(bf), kk)
        return v

    def st_lmat(v):
        gc2 = jnp.where(left, v['gc_f'], v['gc_b'])
        gc_row = jnp.sum(jnp.where(diag2, gc2, 0.0), axis=0, keepdims=True)
        decay2 = jnp.where(keep2, jnp.exp(jnp.where(keep2, gc2 - gc_row, 0.0)), 0.0)
        l2 = jnp.where(strict2, jnp.where(left, v['beta_f'], v['beta_b']) * v['gram2'] * decay2, 0.0)
        ld = jnp.where(same2, l2, 0.0)
        v.update(qkm=(v['qk2'] * decay2).astype(bf), lo=l2 - ld, p=eye2 - ld,
                 m=_dot(ld.astype(bf), blockdiag(ld)))
        del v['gram2'], v['qk2']
        return v

    def st_level(v):
        r = _dot(jnp.concatenate([v['p'], v['m']], axis=0).astype(bf), blockdiag(v['m']))
        v.update(p=v['p'] + r[:CHUNK], m=r[CHUNK:])
        return v

    def st_dinv(v):
        v['d_inv'] = v['p'] + _dot(v['p'].astype(bf), blockdiag(v['m']))
        return v

    def st_n(v):
        v['n'] = _dot(v['d_inv'].astype(bf), blockdiag(v['lo']))
        return v

    def st_nsq(v):
        v['nsq'] = _dot(v['n'].astype(bf), blockdiag(v['n']))
        return v

    def st_pn(v):
        pn = eye2 - v['n']
        v['pn'] = pn + _dot(pn.astype(bf), blockdiag(v['nsq']))
        return v

    def st_t(v):
        v['t2'] = _dot(v['pn'].astype(bf), blockdiag(v['d_inv']))
        return v

    def st_uw(v):
        kc, vc = v['kc'], v['vc']
        v['egc_f'] = jnp.exp(v['gc_f'])
        v['egc_b'] = jnp.exp(v['gc_b'])
        rhs = jnp.concatenate([
            jnp.concatenate([vc * v['beta_f'], kc * (v['beta_f'] * v['egc_f'])], axis=1),
            jnp.concatenate([vc * v['beta_b'], kc * (v['beta_b'] * v['egc_b'])], axis=1)], axis=0).astype(bf)
        v['uw'] = _dot(blockdiag(v['t2']), rhs)
        return v

    def st_store(v):
        h, c, rows, uw, qc, kc = v['h'], v['c'], v['rows'], v['uw'], v['qc'], v['kc']
        r2 = pl.multiple_of(c * 2 * CHUNK, 2 * CHUNK)
        for d, egc in ((0, v['egc_f']), (1, v['egc_b'])):
            u_ref[d, 0, h, rows, :] = uw[d * CHUNK:(d + 1) * CHUNK, :LANES].astype(bf)
            wq_ref[d, 0, h, pl.ds(r2, CHUNK), :] = uw[d * CHUNK:(d + 1) * CHUNK, LANES:].astype(bf)
            wq_ref[d, 0, h, pl.ds(r2 + CHUNK, CHUNK), :] = (qc * egc).astype(bf)
        gtot_f = v['gc_f'][CHUNK - 1:CHUNK, :]
        gtot_b = v['gc_b'][0:1, :]
        kd = jnp.concatenate([kc * jnp.exp(gtot_f - v['gc_f']), kc * jnp.exp(gtot_b - v['gc_b'])], axis=0)
        r3_ = pl.multiple_of(c * 3 * CHUNK, CHUNK)
        qkkd_ref[0, h, pl.ds(r3_, CHUNK), :] = v['qkm']
        qkkd_ref[0, h, pl.ds(r3_ + CHUNK, 2 * CHUNK), :] = kd.T.astype(bf)
        r8 = pl.multiple_of(c * SUBLANES, SUBLANES)
        gl_ref[0, 0, h, pl.ds(r8, SUBLANES), :] = jnp.broadcast_to(jnp.exp(gtot_f), (SUBLANES, LANES))
        gl_ref[1, 0, h, pl.ds(r8, SUBLANES), :] = jnp.broadcast_to(jnp.exp(gtot_b), (SUBLANES, LANES))

    levels = [st_level] * (int(math.log2(INV_BLOCK)) - 2)
    stages = [st_lmat] + levels + [st_dinv, st_n, st_nsq, st_pn, st_t, st_uw]

    for h in range(DN_HEADS):
        qs_ref[h] = l2n(conv_silu(h, q_ref, qp_ref, qn_ref, cw[0], xq_ref)) * (DN_HEAD_DIM ** -0.5)
        ks_ref[h] = l2n(conv_silu(h, k_ref, kp_ref, kn_ref, cw[1], xk_ref))
        vs_ref[h] = conv_silu(h, v_ref, vp_ref, vn_ref, cw[2], xv_ref)
        bf_ref[h] = pick(beta_all, h)
        bb_ref[h] = pick(beta_all, DN_HEADS + h)
        gf_ref[h] = pick(gcf, 2 * DN_HEADS + h)
        gb_ref[h] = pick(gcb, 3 * DN_HEADS + h)

        def trip(j, carry, h=h):
            vals = [st_load(h, j * GDN_UNROLL + k) for k in range(GDN_UNROLL)]
            for stage in stages:
                vals = [stage(v) for v in vals]
            for v in vals:
                st_store(v)
            return carry

        lax.fori_loop(0, t // (CHUNK * GDN_UNROLL), trip, 0)


def _gdn_prep(proj, ba, conv_w, a_log_row, dt_bias_row, t):
    b, s, _ = proj.shape
    nh = DN_HEADS
    nblk = s // t
    n = s // CHUNK
    tpb = t // HALO

    wcol = DN_WIDTH // LANES

    def main(base):
        return pl.BlockSpec((1, t, DN_WIDTH), lambda bi, i: (bi, i, base // wcol))

    def prev(base):
        return pl.BlockSpec((1, HALO, DN_WIDTH),
                            lambda bi, i: (bi, jnp.maximum(i * tpb - 1, 0), base // wcol))

    def nxt(base):
        return pl.BlockSpec((1, HALO, DN_WIDTH),
                            lambda bi, i: (bi, jnp.minimum((i + 1) * tpb, s // HALO - 1), base // wcol))

    in_specs = []
    for base in (COL_DN_Q, COL_DN_K, COL_DN_V):
        in_specs += [main(base), prev(base), nxt(base)]
    in_specs += [
        pl.BlockSpec((3, CONV_K, DN_WIDTH), lambda bi, i: (0, 0, 0)),
        pl.BlockSpec((1, t, LANES), lambda bi, i: (bi, i, 0)),
        pl.BlockSpec((1, LANES), lambda bi, i: (0, 0)),
        pl.BlockSpec((1, LANES), lambda bi, i: (0, 0)),
    ]
    out_specs = [
        pl.BlockSpec((2, 1, nh, 2 * t, LANES), lambda bi, i: (0, bi, 0, i, 0)),
        pl.BlockSpec((2, 1, nh, t, LANES), lambda bi, i: (0, bi, 0, i, 0)),
        pl.BlockSpec((1, nh, 3 * t, LANES), lambda bi, i: (bi, 0, i, 0)),
        pl.BlockSpec((2, 1, nh, t // CHUNK * SUBLANES, LANES), lambda bi, i: (0, bi, 0, i, 0)),
    ]
    out_shape = [
        jax.ShapeDtypeStruct((2, b, nh, 2 * s, LANES), jnp.bfloat16),
        jax.ShapeDtypeStruct((2, b, nh, s, LANES), jnp.bfloat16),
        jax.ShapeDtypeStruct((b, nh, 3 * s, LANES), jnp.bfloat16),
        jax.ShapeDtypeStruct((2, b, nh, n * SUBLANES, LANES), jnp.float32),
    ]
    proj9 = [proj] * 9
    return pl.pallas_call(
        functools.partial(_gdn_prep_kernel, t=t, nblk=nblk),
        grid=(b, nblk),
        in_specs=in_specs,
        out_specs=out_specs,
        out_shape=out_shape,
        scratch_shapes=([pltpu.VMEM((nh, t, LANES), jnp.float32)] * 7
                        + [pltpu.VMEM((nh, t + 2 * HALO, LANES), jnp.float32)] * 3),
        compiler_params=_params("parallel", "parallel"),
        name="gdn_prep",
    )(*proj9, conv_w, ba, a_log_row, dt_bias_row)


def _gdn_scan_kernel(wq_f, u_f, qkkd_f, gl_f, wq_b, u_b, qkkd_b, gl_b, o_f, o_b, state_ref, *, t):
    nc = t // CHUNK
    nb = u_f.shape[1]

    @pl.when(pl.program_id(0) == 0)
    def _():
        state_ref[...] = jnp.zeros_like(state_ref)

    streams = ((0, wq_f, u_f, qkkd_f, gl_f, o_f), (1, wq_b, u_b, qkkd_b, gl_b, o_b))
    zeros = jnp.zeros((CHUNK, DN_HEAD_DIM), jnp.bfloat16)

    def body(c, carry):
        work = []
        for d, wq, u, qkkd, gl, o in streams:
            cc = c if d == 0 else nc - 1 - c
            r2 = pl.multiple_of(cc * 2 * CHUNK, 2 * CHUNK)
            for bi in range(nb):
                for hh in range(DN_HEADS):
                    st = state_ref[d, bi, hh]
                    ws = _dot(wq[0, bi, hh, pl.ds(r2, 2 * CHUNK), :], st.astype(jnp.bfloat16))
                    work.append((d, bi, hh, cc, u, qkkd, gl, o, st, ws))
        for d, bi, hh, cc, u, qkkd, gl, o, st, ws in work:
            r0 = pl.multiple_of(cc * CHUNK, CHUNK)
            r3 = pl.multiple_of(cc * 3 * CHUNK, CHUNK)
            r8 = pl.multiple_of(cc * SUBLANES, SUBLANES)
            v_new = (u[0, bi, hh, pl.ds(r0, CHUNK), :].astype(jnp.float32) - ws[:CHUNK]).astype(jnp.bfloat16)
            rhs = jnp.concatenate([v_new, zeros] if d == 0 else [zeros, v_new], axis=0)
            r = _dot(qkkd[bi, hh, pl.ds(r3, 3 * CHUNK), :], rhs)
            o[bi, hh, pl.ds(r0, CHUNK), :] = (ws[CHUNK:] + r[:CHUNK]).astype(o.dtype)
            decay = gl[0, bi, hh, pl.ds(r8, SUBLANES), :][0:1]
            state_ref[d, bi, hh] = st * decay + r[CHUNK:]
        return carry

    lax.fori_loop(0, nc, body, 0)


def _gdn_scan(wq, u, qkkd, gl, t):
    _, b, nh, s, _ = u.shape
    nblk = s // t
    g8 = t // CHUNK * SUBLANES

    def specs(d):
        blk = (lambda i: i) if d == 0 else (lambda i: nblk - 1 - i)
        idx = lambda i: (d, 0, 0, blk(i), 0)
        idx4 = lambda i: (0, 0, blk(i), 0)
        return [
            pl.BlockSpec((1, b, nh, 2 * t, LANES), idx),
            pl.BlockSpec((1, b, nh, t, LANES), idx),
            pl.BlockSpec((b, nh, 3 * t, LANES), idx4),
            pl.BlockSpec((1, b, nh, g8, LANES), idx),
        ], pl.BlockSpec((b, nh, t, LANES), idx4)

    in_f, out_f = specs(0)
    in_b, out_b = specs(1)
    o_shape = jax.ShapeDtypeStruct((b, nh, s, LANES), jnp.bfloat16)
    return pl.pallas_call(
        functools.partial(_gdn_scan_kernel, t=t),
        grid=(nblk,),
        in_specs=in_f + in_b,
        out_specs=[out_f, out_b],
        out_shape=[o_shape, o_shape],
        scratch_shapes=[pltpu.VMEM((2, b, nh, DN_HEAD_DIM, DN_HEAD_DIM), jnp.float32)],
        compiler_params=_params("arbitrary"),
        name="gdn_scan",
    )(wq, u, qkkd, gl, wq, u, qkkd, gl)


def _merge_kernel(of_ref, ob_ref, z_ref, ga_ref, gb_ref, ya_ref, x_ref, gt_ref, og_ref,
                  wa_ref, wb_ref, wo_ref, o_ref):
    og = og_ref[...]
    z = z_ref[0].astype(jnp.float32)
    parts = []
    for hh in range(DN_HEADS):
        o = of_ref[0, hh].astype(jnp.float32) + ob_ref[0, hh].astype(jnp.float32)
        y = o * lax.rsqrt(jnp.mean(o * o, axis=-1, keepdims=True) + EPS) * og
        parts.append((y * _silu(z[:, hh * LANES:(hh + 1) * LANES])).astype(jnp.bfloat16))
    yb = jnp.concatenate(parts, axis=1)
    pa = _dot(ya_ref[0], wa_ref[...])
    pb = _dot(yb, wb_ref[...])
    merged = (jax.nn.sigmoid(ga_ref[0].astype(jnp.float32)) * pa
              + jax.nn.sigmoid(gb_ref[0].astype(jnp.float32)) * pb)
    o_ref[0] = x_ref[0] + gt_ref[0] * _dot(merged.astype(jnp.bfloat16), wo_ref[...])


def _merge(o_f, o_b, proj, y_a, x, gate, out_gain, w_a, w_b, w_o, tm):
    b, s, d = x.shape
    nh = DN_HEADS
    zw = DN_WIDTH // LANES
    return pl.pallas_call(
        _merge_kernel,
        grid=(b, s // tm),
        in_specs=[
            pl.BlockSpec((1, nh, tm, LANES), lambda bi, i: (bi, 0, i, 0)),
            pl.BlockSpec((1, nh, tm, LANES), lambda bi, i: (bi, 0, i, 0)),
            pl.BlockSpec((1, tm, DN_WIDTH), lambda bi, i: (bi, i, COL_DN_Z // zw)),
            pl.BlockSpec((1, tm, d), lambda bi, i: (bi, i, 0)),
            pl.BlockSpec((1, tm, d), lambda bi, i: (bi, i, 1)),
            pl.BlockSpec((1, tm, DA_WIDTH), lambda bi, i: (bi, i, 0)),
            pl.BlockSpec((1, tm, d), lambda bi, i: (bi, i, 0)),
            pl.BlockSpec((1, 1, d), lambda bi, i: (bi, 0, 0)),
            pl.BlockSpec((1, LANES), lambda bi, i: (0, 0)),
            pl.BlockSpec(w_a.shape, lambda bi, i: (0, 0)),
            pl.BlockSpec(w_b.shape, lambda bi, i: (0, 0)),
            pl.BlockSpec(w_o.shape, lambda bi, i: (0, 0)),
        ],
        out_specs=pl.BlockSpec((1, tm, d), lambda bi, i: (bi, i, 0)),
        out_shape=jax.ShapeDtypeStruct((b, s, d), jnp.float32),
        compiler_params=_params("parallel", "parallel"),
        name="merge",
    )(o_f, o_b, proj, proj, proj, y_a, x, gate, out_gain, w_a, w_b, w_o)


def _mlp_kernel(x_ref, g_ref, sc_ref, sh_ref, gt_ref, w1_ref, w2_ref, o_ref, h_ref, acc_ref):
    j = pl.program_id(2)

    @pl.when(j == 0)
    def _():
        h_ref[...] = _modulated_norm(x_ref[0], g_ref[...], sc_ref[0], sh_ref[0]).astype(h_ref.dtype)
        acc_ref[...] = jnp.zeros_like(acc_ref)

    a = jnp.maximum(_dot(h_ref[...], w1_ref[...]), 0.0)
    acc_ref[...] += _dot((a * a).astype(jnp.bfloat16), w2_ref[...])

    @pl.when(j == pl.num_programs(2) - 1)
    def _():
        o_ref[0] = x_ref[0] + gt_ref[0] * acc_ref[...]


def _mlp(x, gain, scale, shift, gate, w1, w2, tm, tf):
    b, s, d = x.shape
    f = w1.shape[1]
    vec = pl.BlockSpec((1, 1, d), lambda bi, i, j: (bi, 0, 0))
    return pl.pallas_call(
        _mlp_kernel,
        grid=(b, s // tm, f // tf),
        in_specs=[
            pl.BlockSpec((1, tm, d), lambda bi, i, j: (bi, i, 0)),
            pl.BlockSpec((1, d), lambda bi, i, j: (0, 0)),
            vec, vec, vec,
            pl.BlockSpec((d, tf), lambda bi, i, j: (0, j)),
            pl.BlockSpec((tf, d), lambda bi, i, j: (j, 0)),
        ],
        out_specs=pl.BlockSpec((1, tm, d), lambda bi, i, j: (bi, i, 0)),
        out_shape=jax.ShapeDtypeStruct((b, s, d), jnp.float32),
        scratch_shapes=[pltpu.VMEM((tm, d), jnp.bfloat16), pltpu.VMEM((tm, d), jnp.float32)],
        compiler_params=_params("parallel", "parallel", "arbitrary"),
        name="mlp",
    )(x, gain, scale, shift, gate, w1, w2)


def _reorder_w_in(w_in, d_model):
    assert d_model == D_MODEL
    main = 3 * DA_WIDTH + 4 * DN_WIDTH
    small = 4 * DN_HEADS
    head = w_in[:, :main]
    ba = w_in[:, main:main + small]
    gates = w_in[:, main + small:]
    pad = jnp.zeros((w_in.shape[0], LANES - small), w_in.dtype)
    return (jnp.concatenate([gates, head], axis=1).astype(jnp.bfloat16),
            jnp.concatenate([ba, pad], axis=1).astype(jnp.bfloat16))


def _rope_tables(seq):
    pos = jnp.arange(seq, dtype=jnp.float32)
    inv = 1.0 / (ROPE_THETA ** (jnp.arange(0, DA_HEAD_DIM, 2, dtype=jnp.float32) / DA_HEAD_DIM))
    ang = pos[:, None] * inv[None, :]
    ang = jnp.concatenate([ang, ang, ang, ang], axis=-1)
    sign = jnp.where((jnp.arange(LANES) % DA_HEAD_DIM) < DA_HEAD_DIM // 2, -1.0, 1.0)
    return jnp.cos(ang), jnp.sin(ang) * sign


def kernel(x, c, ada_w, ada_b, norm1_g, norm2_g, w_in, dn_conv, dn_A_log, dn_dt_bias, dn_out_gain,
           qk_norm_gain, diff_lambda, diff_subln_gain, w_branch_a, w_branch_b, w_out, w_mlp1, w_mlp2):
    b, s, d = x.shape
    depth = ada_w.shape[0]
    assert s % 512 == 0 and d % LANES == 0 and b <= SUBLANES
    tm = min(1024, s)
    t_gdn = min(512, s)
    tq = min(512, s)
    tk = min(256, s // ATTN_UNROLL)

    c_pad = jnp.zeros((SUBLANES, d), jnp.float32).at[:b].set(c)
    mod = _adaln(c_pad, ada_w, ada_b)[:, :b]
    cos, sin_signed = _rope_tables(s)
    small = 4 * DN_HEADS

    for l in range(depth):
        sh1, sc1, gt1, sh2, sc2, gt2 = [mod[l, :, None, k * d:(k + 1) * d] for k in range(6)]
        lambda_init = 0.8 - 0.6 * math.exp(-0.3 * l)

        w_in_l, w_ba_l = _reorder_w_in(w_in[l], d)
        proj, ba = _inproj(x, norm1_g[l][None], sc1, sh1, w_in_l, w_ba_l, min(512, s))

        qgain = jnp.tile(qk_norm_gain[l, 0], 2)[None]
        kgain = jnp.tile(qk_norm_gain[l, 1], 2)[None]
        qt, k, vt = _attn_prep(proj, cos, sin_signed, qgain, kgain, min(1024, s))
        y_a = _attn(qt, k, vt, diff_lambda[l], diff_subln_gain[l][None], lambda_init, tq, tk)

        conv_w = dn_conv[l].reshape(CONV_K, 3, DN_WIDTH).transpose(1, 0, 2)
        row = jnp.zeros((1, LANES), jnp.float32)
        a_log_row = row.at[0, small // 2:small].set(dn_A_log[l].reshape(-1))
        dt_bias_row = row.at[0, small // 2:small].set(dn_dt_bias[l].reshape(-1))
        wq, u, qkkd, gl = _gdn_prep(proj, ba, conv_w, a_log_row, dt_bias_row, t_gdn)
        o_f, o_b = _gdn_scan(wq, u, qkkd, gl, t_gdn)

        x = _merge(o_f, o_b, proj, y_a, x, gt1, dn_out_gain[l][None],
                   w_branch_a[l].astype(jnp.bfloat16), w_branch_b[l].astype(jnp.bfloat16),
                   w_out[l].astype(jnp.bfloat16), min(512, s))
        x = _mlp(x, norm2_g[l][None], sc2, sh2, gt2,
                 w_mlp1[l].astype(jnp.bfloat16), w_mlp2[l].astype(jnp.bfloat16), tm, 1024)
    return x
```

```python
import functools
import math
from typing import NamedTuple

import jax
import jax.numpy as jnp
from jax import lax
from jax.experimental import pallas as pl
from jax.experimental.pallas import tpu as pltpu

DA_HEADS = 4
DA_HEAD_DIM = 64
DA_WIDTH = DA_HEADS * 2 * DA_HEAD_DIM
DN_HEADS = 4
DN_HEAD_DIM = 128
DN_WIDTH = DN_HEADS * DN_HEAD_DIM
CONV_K = 5
CHUNK = 64
ROPE_THETA = 10000.0
EPS = 1e-6
LOG2_E = math.log2(math.e)

LANES = 128
SUBLANES = 8
VMEM_LIMIT_BYTES = 56 * 1024 * 1024

D_MODEL = 1024
COL_GATE = 0
COL_DA_Q = 2 * D_MODEL // LANES
COL_DA_K = COL_DA_Q + DA_WIDTH // LANES
COL_DA_V = COL_DA_K + DA_WIDTH // LANES
COL_DN_Q = COL_DA_V + DA_WIDTH // LANES
COL_DN_K = COL_DN_Q + DN_WIDTH // LANES
COL_DN_V = COL_DN_K + DN_WIDTH // LANES
COL_DN_Z = COL_DN_V + DN_WIDTH // LANES


def _params(*sem):
    return pltpu.CompilerParams(dimension_semantics=sem, vmem_limit_bytes=VMEM_LIMIT_BYTES)


def _layer_spec(tail, layer, **kw):
    zeros = (0,) * len(tail)
    return pl.BlockSpec((None,) + tuple(tail), lambda *_: (layer,) + zeros, **kw)


def _mod_spec(d, layer, k):
    return pl.BlockSpec((None, None, None, 1, d), lambda bi, *_: (layer, bi, k, 0, 0))


MOD_SHIFT1, MOD_SCALE1, MOD_GATE1, MOD_SHIFT2, MOD_SCALE2, MOD_GATE2 = range(6)


def _silu(x):
    return x * jax.nn.sigmoid(x)


def _softplus(x):
    return jnp.maximum(x, 0.0) + jnp.log(1.0 + jnp.exp(-jnp.abs(x)))


def _dot(a, b):
    return jnp.dot(a, b, preferred_element_type=jnp.float32)


def _dot_nt(a, b):
    return lax.dot_general(a, b, (((1,), (1,)), ((), ())), preferred_element_type=jnp.float32)


def _split3(x):
    hi = x.astype(jnp.bfloat16).astype(jnp.float32)
    r1 = x - hi
    mid = r1.astype(jnp.bfloat16).astype(jnp.float32)
    lo = (r1 - mid).astype(jnp.bfloat16).astype(jnp.float32)
    return hi, mid, lo


def _adaln_kernel(c_ref, w_ref, b_ref, o_ref):
    c_act = _silu(c_ref[...])
    o_ref[0] = jnp.dot(c_act, w_ref[0], preferred_element_type=jnp.float32,
                       precision=lax.Precision.HIGHEST) + b_ref[0]


def _adaln(c_pad, ada_w, ada_b):
    depth, d, n = ada_w.shape
    tn = 1536
    return pl.pallas_call(
        _adaln_kernel,
        grid=(depth, n // tn),
        in_specs=[
            pl.BlockSpec((SUBLANES, d), lambda l, j: (0, 0)),
            pl.BlockSpec((1, d, tn), lambda l, j: (l, 0, j)),
            pl.BlockSpec((1, 1, tn), lambda l, j: (l, 0, j)),
        ],
        out_specs=pl.BlockSpec((1, SUBLANES, tn), lambda l, j: (l, 0, j)),
        out_shape=jax.ShapeDtypeStruct((depth, SUBLANES, n), jnp.float32),
        compiler_params=_params("parallel", "parallel"),
        name="adaln",
    )(c_pad, ada_w, ada_b.reshape(depth, 1, n))


def _modulated_norm(x, gain, scale, shift):
    y = x * lax.rsqrt(jnp.mean(x * x, axis=-1, keepdims=True) + EPS)
    return (y * gain) * (1.0 + scale) + shift


INPROJ_COLS = 512


def _inproj_kernel(x_ref, g_ref, sc_ref, sh_ref, w_ref, wba_ref, o_ref, ba_ref):
    h = _modulated_norm(x_ref[0], g_ref[...], sc_ref[...], sh_ref[...]).astype(jnp.bfloat16)
    for c in range(0, w_ref.shape[1], INPROJ_COLS):
        o_ref[0, :, c:c + INPROJ_COLS] = _dot(h, w_ref[:, c:c + INPROJ_COLS]).astype(o_ref.dtype)
    ba_ref[0] = _dot(h, wba_ref[...])


def _inproj(x, gains, mod, w, w_ba, layer, tm):
    b, s, d = x.shape
    n = w.shape[2]
    assert n % INPROJ_COLS == 0
    resident = dict(pipeline_mode=pl.Buffered(1))
    return pl.pallas_call(
        _inproj_kernel,
        grid=(b, s // tm),
        in_specs=[
            pl.BlockSpec((1, tm, d), lambda bi, i: (bi, i, 0)),
            _layer_spec((1, d), layer),
            _mod_spec(d, layer, MOD_SCALE1),
            _mod_spec(d, layer, MOD_SHIFT1),
            _layer_spec((d, n), layer, **resident),
            _layer_spec((d, LANES), layer, **resident),
        ],
        out_specs=[pl.BlockSpec((1, tm, n), lambda bi, i: (bi, i, 0)),
                   pl.BlockSpec((1, tm, LANES), lambda bi, i: (bi, i, 0))],
        out_shape=[jax.ShapeDtypeStruct((b, s, n), jnp.bfloat16),
                   jax.ShapeDtypeStruct((b, s, LANES), jnp.float32)],
        compiler_params=_params("parallel", "parallel"),
        name="inproj",
    )(x, gains, mod, mod, w, w_ba)


def _attn_prep_kernel(q_ref, k_ref, v_ref, cos_ref, sin_ref, qg_ref, kg_ref, qo_ref, kt_ref, vo_ref):
    lane = lax.broadcasted_iota(jnp.int32, (1, LANES), 1)
    left = lane < DA_HEAD_DIM
    first_half = (lane % DA_HEAD_DIM) < (DA_HEAD_DIM // 2)
    cos = cos_ref[...]
    sin = sin_ref[...]

    lane_col = lax.broadcasted_iota(jnp.int32, (LANES, 1), 0)
    same_map = ((lane_col < DA_HEAD_DIM) == left).astype(jnp.bfloat16)

    def norm_rope(x, gain):
        sq = x * x
        hi = sq.astype(jnp.bfloat16)
        lo = (sq - hi.astype(jnp.float32)).astype(jnp.bfloat16)
        ms = (_dot(hi, same_map) + _dot(lo, same_map)) * (1.0 / DA_HEAD_DIM)
        y = x * lax.rsqrt(ms + EPS) * gain
        fwd = pltpu.roll(y, LANES - DA_HEAD_DIM // 2, axis=1)
        bwd = pltpu.roll(y, DA_HEAD_DIM // 2, axis=1)
        return y * cos + jnp.where(first_half, fwd, bwd) * sin

    q = norm_rope(q_ref[0].astype(jnp.float32), qg_ref[...]) * (DA_HEAD_DIM ** -0.5 * LOG2_E)
    k = norm_rope(k_ref[0].astype(jnp.float32), kg_ref[...])
    qo_ref[0, 0, 0] = jnp.where(left, q, 0.0).astype(qo_ref.dtype)
    qo_ref[0, 0, 1] = jnp.where(left, 0.0, q).astype(qo_ref.dtype)
    kt_ref[0, 0] = k.T.astype(kt_ref.dtype)
    v = v_ref[0].astype(vo_ref.dtype)
    vo_ref[0, 0] = jnp.concatenate([v, jnp.ones_like(v)], axis=1)


def _attn_prep(proj, cos, sin_signed, qk_gains, layer, ts):
    b, s, _ = proj.shape
    h = DA_HEADS
    col = lambda base: (lambda bi, hi, i: (bi, i, base + hi))
    return pl.pallas_call(
        _attn_prep_kernel,
        grid=(b, h, s // ts),
        in_specs=[
            pl.BlockSpec((1, ts, LANES), col(COL_DA_Q)),
            pl.BlockSpec((1, ts, LANES), col(COL_DA_K)),
            pl.BlockSpec((1, ts, LANES), col(COL_DA_V)),
            pl.BlockSpec((ts, LANES), lambda bi, hi, i: (i, 0)),
            pl.BlockSpec((ts, LANES), lambda bi, hi, i: (i, 0)),
            pl.BlockSpec((None, None, 1, LANES), lambda bi, hi, i: (layer, 0, 0, 0)),
            pl.BlockSpec((None, None, 1, LANES), lambda bi, hi, i: (layer, 1, 0, 0)),
        ],
        out_specs=[
            pl.BlockSpec((1, 1, 2, ts, LANES), lambda bi, hi, i: (bi, hi, 0, i, 0)),
            pl.BlockSpec((1, 1, LANES, ts), lambda bi, hi, i: (bi, hi, 0, i)),
            pl.BlockSpec((1, 1, ts, 2 * LANES), lambda bi, hi, i: (bi, hi, i, 0)),
        ],
        out_shape=[
            jax.ShapeDtypeStruct((b, h, 2, s, LANES), jnp.bfloat16),
            jax.ShapeDtypeStruct((b, h, LANES, s), jnp.bfloat16),
            jax.ShapeDtypeStruct((b, h, s, 2 * LANES), jnp.bfloat16),
        ],
        compiler_params=_params("parallel", "parallel", "parallel"),
        name="attn_prep",
    )(proj, proj, proj, cos, sin_signed, qk_gains, qk_gains)


ATTN_UNROLL = 8
ATTN_ROW_BLOCK = 32


def _attn_kernel(q_ref, kt_ref, v_ref, lam_ref, g_ref, o_ref, s_ref, p_ref, m_ref, alpha_ref, acc_ref,
                 *, tq, tk, lambda_init):
    s_len = v_ref.shape[2]
    q = q_ref[0, 0].reshape(2 * tq, LANES)
    n_kv = s_len // tk
    groups = tk // LANES
    rb = ATTN_ROW_BLOCK

    m_ref[...] = jnp.full(m_ref.shape, -jnp.inf, jnp.float32)
    acc_ref[...] = jnp.zeros(acc_ref.shape, jnp.float32)

    def scores(j, slot):
        s_ref[slot] = _dot(q, kt_ref[0, 0, :, pl.ds(pl.multiple_of(j * tk, tk), tk)])

    def update(j, slot):
        for r in range(2 * tq // rb):
            rows = slice(r * rb, (r + 1) * rb)
            sg = [s_ref[slot, rows, g * LANES:(g + 1) * LANES] for g in range(groups)]
            m_old = m_ref[rows, :]
            m_new = jnp.maximum(m_old, jnp.max(functools.reduce(jnp.maximum, sg), axis=-1, keepdims=True))
            alpha_ref[rows, :] = jnp.exp2(m_old - m_new)
            m_ref[rows, :] = m_new
            for g in range(groups):
                p_ref[rows, g * LANES:(g + 1) * LANES] = jnp.exp2(sg[g] - m_new).astype(jnp.bfloat16)
        alpha = alpha_ref[...]
        pv = _dot(p_ref[...], v_ref[0, 0, pl.ds(pl.multiple_of(j * tk, tk), tk), :])
        acc_ref[...] = jnp.concatenate([alpha, alpha], axis=1) * acc_ref[...] + pv

    def body(jj, carry):
        j = ATTN_UNROLL * jj
        for k in range(ATTN_UNROLL):
            scores(jnp.minimum(j + k + 1, n_kv - 1), (k + 1) % 2)
            update(j + k, k % 2)
        return carry

    scores(0, 0)
    lax.fori_loop(0, n_kv // ATTN_UNROLL, body, 0)
    o = acc_ref[:, :LANES] / acc_ref[:, LANES:]
    lq = lam_ref[...]
    lam = (jnp.exp(jnp.sum(lq[0:1] * lq[1:2], axis=-1, keepdims=True))
           - jnp.exp(jnp.sum(lq[2:3] * lq[3:4], axis=-1, keepdims=True)) + lambda_init)
    d = o[:tq] - lam * o[tq:]
    y = d * lax.rsqrt(jnp.mean(d * d, axis=-1, keepdims=True) + EPS) * g_ref[...]
    o_ref[0] = (y * (1.0 - lambda_init)).astype(o_ref.dtype)


def _attn(qs, kt, v, lam_params, subln_gain, layer, lambda_init, tq, tk):
    b, h, _, s, _ = qs.shape
    assert s % (tk * ATTN_UNROLL) == 0 and ATTN_UNROLL % 2 == 0 and (2 * tq) % ATTN_ROW_BLOCK == 0
    return pl.pallas_call(
        functools.partial(_attn_kernel, tq=tq, tk=tk, lambda_init=lambda_init),
        grid=(b, h, s // tq),
        in_specs=[
            pl.BlockSpec((1, 1, 2, tq, LANES), lambda bi, hi, i: (bi, hi, 0, i, 0)),
            pl.BlockSpec((1, 1, LANES, s), lambda bi, hi, i: (bi, hi, 0, 0)),
            pl.BlockSpec((1, 1, s, 2 * LANES), lambda bi, hi, i: (bi, hi, 0, 0)),
            _layer_spec(lam_params.shape[1:], layer),
            _layer_spec((1, LANES), layer),
        ],
        out_specs=pl.BlockSpec((1, tq, LANES), lambda bi, hi, i: (bi, i, hi)),
        out_shape=jax.ShapeDtypeStruct((b, s, h * LANES), jnp.bfloat16),
        scratch_shapes=[pltpu.VMEM((2, 2 * tq, tk), jnp.float32), pltpu.VMEM((2 * tq, tk), jnp.bfloat16),
                        pltpu.VMEM((2 * tq, LANES), jnp.float32), pltpu.VMEM((2 * tq, LANES), jnp.float32),
                        pltpu.VMEM((2 * tq, 2 * LANES), jnp.float32)],
        compiler_params=_params("parallel", "parallel", "parallel"),
        name="attn",
    )(qs, kt, v, lam_params, subln_gain)


INV_BLOCK = 16
HALO = 16
GDN_UNROLL = 8


def _gdn_prep_kernel(q_ref, qp_ref, qn_ref, k_ref, kp_ref, kn_ref, v_ref, vp_ref, vn_ref,
                     cw_ref, ba_ref, alog_ref, dtb_ref,
                     wq_ref, u_ref, qkkd_ref, gl_ref,
                     qs_ref, ks_ref, vs_ref, bf_ref, bb_ref, gf_ref, gb_ref, xq_ref, xk_ref, xv_ref,
                     *, t, nblk):
    i = pl.program_id(1)
    pad = CONV_K // 2
    bf = jnp.bfloat16

    def conv_silu(h, x_ref, p_ref, n_ref, w, xp_ref):
        f32 = jnp.float32
        cols = slice(h * LANES, (h + 1) * LANES)
        xp_ref[h, 0:HALO, :] = jnp.where(i == 0, 0.0, p_ref[0, :, cols].astype(f32))
        xp_ref[h, HALO:HALO + t, :] = x_ref[0, :, cols].astype(f32)
        xp_ref[h, HALO + t:2 * HALO + t, :] = jnp.where(i == nblk - 1, 0.0, n_ref[0, :, cols].astype(f32))
        acc = xp_ref[h, HALO - pad:HALO - pad + t, :] * w[0:1, cols]
        for j in range(1, CONV_K):
            acc = acc + xp_ref[h, HALO - pad + j:HALO - pad + j + t, :] * w[j:j + 1, cols]
        return _silu(acc)

    def l2n(x):
        return x * lax.rsqrt(jnp.sum(x * x, axis=-1, keepdims=True) + EPS)

    cw = cw_ref[...]
    lane = lax.broadcasted_iota(jnp.int32, (1, LANES), 1)

    r3 = lax.broadcasted_iota(jnp.int32, (CHUNK, 3 * CHUNK), 0)
    c3 = lax.broadcasted_iota(jnp.int32, (CHUNK, 3 * CHUNK), 1) % CHUNK
    tril3 = (r3 >= c3).astype(bf)
    triu3 = (r3 <= c3).astype(bf)

    ba = ba_ref[0]
    g_all = -jnp.exp(alog_ref[...]) * _softplus(ba + dtb_ref[...])
    beta_all = jax.nn.sigmoid(ba)
    hi, mid, lo = _split3(g_all)
    gcf, gcb = [], []
    for c in range(t // CHUNK):
        rows = slice(c * CHUNK, (c + 1) * CHUNK)
        parts = jnp.concatenate([hi[rows], mid[rows], lo[rows]], axis=0).astype(bf)
        gcf.append(_dot(tril3, parts))
        gcb.append(_dot(triu3, parts))

    def pick(x, idx):
        col = jnp.sum(jnp.where(lane == idx, x, 0.0), axis=-1, keepdims=True)
        return jnp.broadcast_to(col, (t, LANES))

    gcf = jnp.concatenate(gcf, axis=0)
    gcb = jnp.concatenate(gcb, axis=0)

    row = lax.broadcasted_iota(jnp.int32, (CHUNK, 2 * CHUNK), 0)
    col2 = lax.broadcasted_iota(jnp.int32, (CHUNK, 2 * CHUNK), 1)
    left = col2 < CHUNK
    colm = col2 % CHUNK
    keep2 = (left & (row >= colm)) | (~left & (row <= colm))
    strict2 = keep2 & (row != colm)
    diag2 = row == colm
    eye2 = diag2.astype(jnp.float32)
    same2 = (row // INV_BLOCK) == (colm // INV_BLOCK)
    row4 = lax.broadcasted_iota(jnp.int32, (2 * CHUNK, 2 * CHUNK), 0)
    col4 = lax.broadcasted_iota(jnp.int32, (2 * CHUNK, 2 * CHUNK), 1)
    bd_mask = ((row4 < CHUNK) == (col4 < CHUNK)).astype(bf)

    def blockdiag(y):
        yb = y.astype(bf)
        return jnp.concatenate([yb, yb], axis=0) * bd_mask

    def st_load(h, c):
        rows = pl.ds(pl.multiple_of(c * CHUNK, CHUNK), CHUNK)
        v = dict(h=h, c=c, rows=rows, qc=qs_ref[h, rows, :], kc=ks_ref[h, rows, :], vc=vs_ref[h, rows, :],
                 beta_f=bf_ref[h, rows, :], beta_b=bb_ref[h, rows, :],
                 gc_f=gf_ref[h, rows, :], gc_b=gb_ref[h, rows, :])
        kcb = v['kc'].astype(bf)
        kk = jnp.concatenate([kcb, kcb], axis=0)
        v['gram2'] = _dot_nt(kcb, kk)
        v['qk2'] = _dot_nt(v['qc'].astype(bf), kk)
        return v

    def st_lmat(v):
        gc2 = jnp.where(left, v['gc_f'], v['gc_b'])
        gc_row = jnp.sum(jnp.where(diag2, gc2, 0.0), axis=0, keepdims=True)
        decay2 = jnp.where(keep2, jnp.exp(jnp.where(keep2, gc2 - gc_row, 0.0)), 0.0)
        l2 = jnp.where(strict2, jnp.where(left, v['beta_f'], v['beta_b']) * v['gram2'] * decay2, 0.0)
        ld = jnp.where(same2, l2, 0.0)
        v.update(qkm=(v['qk2'] * decay2).astype(bf), lo=l2 - ld, p=eye2 - ld,
                 m=_dot(ld.astype(bf), blockdiag(ld)))
        del v['gram2'], v['qk2']
        return v

    def st_level(v):
        r = _dot(jnp.concatenate([v['p'], v['m']], axis=0).astype(bf), blockdiag(v['m']))
        v.update(p=v['p'] + r[:CHUNK], m=r[CHUNK:])
        return v

    def st_dinv(v):
        v['d_inv'] = v['p'] + _dot(v['p'].astype(bf), blockdiag(v['m']))
        return v

    def st_n(v):
        v['n'] = _dot(v['d_inv'].astype(bf), blockdiag(v['lo']))
        return v

    def st_nsq(v):
        v['nsq'] = _dot(v['n'].astype(bf), blockdiag(v['n']))
        return v

    def st_pn(v):
        pn = eye2 - v['n']
        v['pn'] = pn + _dot(pn.astype(bf), blockdiag(v['nsq']))
        return v

    def st_t(v):
        v['t2'] = _dot(v['pn'].astype(bf), blockdiag(v['d_inv']))
        return v

    def st_uw(v):
        kc, vc = v['kc'], v['vc']
        v['egc_f'] = jnp.exp(v['gc_f'])
        v['egc_b'] = jnp.exp(v['gc_b'])
        rhs = jnp.concatenate([
            jnp.concatenate([vc * v['beta_f'], kc * (v['beta_f'] * v['egc_f'])], axis=1),
            jnp.concatenate([vc * v['beta_b'], kc * (v['beta_b'] * v['egc_b'])], axis=1)], axis=0).astype(bf)
        v['uw'] = _dot(blockdiag(v['t2']), rhs)
        return v

    def st_store(v):
        h, c, rows, uw, qc, kc = v['h'], v['c'], v['rows'], v['uw'], v['qc'], v['kc']
        r2 = pl.multiple_of(c * 2 * CHUNK, 2 * CHUNK)
        for d, egc in ((0, v['egc_f']), (1, v['egc_b'])):
            u_ref[d, 0, h, rows, :] = uw[d * CHUNK:(d + 1) * CHUNK, :LANES].astype(bf)
            wq_ref[d, 0, h, pl.ds(r2, CHUNK), :] = uw[d * CHUNK:(d + 1) * CHUNK, LANES:].astype(bf)
            wq_ref[d, 0, h, pl.ds(r2 + CHUNK, CHUNK), :] = (qc * egc).astype(bf)
        gtot_f = v['gc_f'][CHUNK - 1:CHUNK, :]
        gtot_b = v['gc_b'][0:1, :]
        kd = jnp.concatenate([kc * jnp.exp(gtot_f - v['gc_f']), kc * jnp.exp(gtot_b - v['gc_b'])], axis=0)
        r3_ = pl.multiple_of(c * 3 * CHUNK, CHUNK)
        qkkd_ref[0, h, pl.ds(r3_, CHUNK), :] = v['qkm']
        qkkd_ref[0, h, pl.ds(r3_ + CHUNK, 2 * CHUNK), :] = kd.T.astype(bf)
        r8 = pl.multiple_of(c * SUBLANES, SUBLANES)
        gl_ref[0, 0, h, pl.ds(r8, SUBLANES), :] = jnp.broadcast_to(jnp.exp(gtot_f), (SUBLANES, LANES))
        gl_ref[1, 0, h, pl.ds(r8, SUBLANES), :] = jnp.broadcast_to(jnp.exp(gtot_b), (SUBLANES, LANES))

    levels = [st_level] * (int(math.log2(INV_BLOCK)) - 2)
    stages = [st_lmat] + levels + [st_dinv, st_n, st_nsq, st_pn, st_t, st_uw]

    for h in range(DN_HEADS):
        qs_ref[h] = l2n(conv_silu(h, q_ref, qp_ref, qn_ref, cw[0], xq_ref)) * (DN_HEAD_DIM ** -0.5)
        ks_ref[h] = l2n(conv_silu(h, k_ref, kp_ref, kn_ref, cw[1], xk_ref))
        vs_ref[h] = conv_silu(h, v_ref, vp_ref, vn_ref, cw[2], xv_ref)
        bf_ref[h] = pick(beta_all, h)
        bb_ref[h] = pick(beta_all, DN_HEADS + h)
        gf_ref[h] = pick(gcf, 2 * DN_HEADS + h)
        gb_ref[h] = pick(gcb, 3 * DN_HEADS + h)

        def trip(j, carry, h=h):
            vals = [st_load(h, j * GDN_UNROLL + k) for k in range(GDN_UNROLL)]
            for stage in stages:
                vals = [stage(v) for v in vals]
            for v in vals:
                st_store(v)
            return carry

        lax.fori_loop(0, t // (CHUNK * GDN_UNROLL), trip, 0)


def _gdn_prep(proj, ba, conv_w, a_log_row, dt_bias_row, layer, t):
    b, s, _ = proj.shape
    nh = DN_HEADS
    nblk = s // t
    n = s // CHUNK
    tpb = t // HALO
    wcol = DN_WIDTH // LANES

    def main(base):
        return pl.BlockSpec((1, t, DN_WIDTH), lambda bi, i: (bi, i, base // wcol))

    def prev(base):
        return pl.BlockSpec((1, HALO, DN_WIDTH),
                            lambda bi, i: (bi, jnp.maximum(i * tpb - 1, 0), base // wcol))

    def nxt(base):
        return pl.BlockSpec((1, HALO, DN_WIDTH),
                            lambda bi, i: (bi, jnp.minimum((i + 1) * tpb, s // HALO - 1), base // wcol))

    in_specs = []
    for base in (COL_DN_Q, COL_DN_K, COL_DN_V):
        in_specs += [main(base), prev(base), nxt(base)]
    in_specs += [
        _layer_spec((3, CONV_K, DN_WIDTH), layer),
        pl.BlockSpec((1, t, LANES), lambda bi, i: (bi, i, 0)),
        _layer_spec((1, LANES), layer),
        _layer_spec((1, LANES), layer),
    ]
    out_specs = [
        pl.BlockSpec((2, 1, nh, 2 * t, LANES), lambda bi, i: (0, bi, 0, i, 0)),
        pl.BlockSpec((2, 1, nh, t, LANES), lambda bi, i: (0, bi, 0, i, 0)),
        pl.BlockSpec((1, nh, 3 * t, LANES), lambda bi, i: (bi, 0, i, 0)),
        pl.BlockSpec((2, 1, nh, t // CHUNK * SUBLANES, LANES), lambda bi, i: (0, bi, 0, i, 0)),
    ]
    out_shape = [
        jax.ShapeDtypeStruct((2, b, nh, 2 * s, LANES), jnp.bfloat16),
        jax.ShapeDtypeStruct((2, b, nh, s, LANES), jnp.bfloat16),
        jax.ShapeDtypeStruct((b, nh, 3 * s, LANES), jnp.bfloat16),
        jax.ShapeDtypeStruct((2, b, nh, n * SUBLANES, LANES), jnp.float32),
    ]
    proj9 = [proj] * 9
    return pl.pallas_call(
        functools.partial(_gdn_prep_kernel, t=t, nblk=nblk),
        grid=(b, nblk),
        in_specs=in_specs,
        out_specs=out_specs,
        out_shape=out_shape,
        scratch_shapes=([pltpu.VMEM((nh, t, LANES), jnp.float32)] * 7
                        + [pltpu.VMEM((nh, t + 2 * HALO, LANES), jnp.float32)] * 3),
        compiler_params=_params("parallel", "parallel"),
        name="gdn_prep",
    )(*proj9, conv_w, ba, a_log_row, dt_bias_row)


def _gdn_scan_kernel(wq_f, u_f, qkkd_f, gl_f, wq_b, u_b, qkkd_b, gl_b, o_f, o_b, state_ref, *, t):
    nc = t // CHUNK
    nb = u_f.shape[1]

    @pl.when(pl.program_id(0) == 0)
    def _():
        state_ref[...] = jnp.zeros_like(state_ref)

    streams = ((0, wq_f, u_f, qkkd_f, gl_f, o_f), (1, wq_b, u_b, qkkd_b, gl_b, o_b))
    zeros = jnp.zeros((CHUNK, DN_HEAD_DIM), jnp.bfloat16)

    def body(c, carry):
        work = []
        for d, wq, u, qkkd, gl, o in streams:
            cc = c if d == 0 else nc - 1 - c
            r2 = pl.multiple_of(cc * 2 * CHUNK, 2 * CHUNK)
            for bi in range(nb):
                for hh in range(DN_HEADS):
                    st = state_ref[d, bi, hh]
                    ws = _dot(wq[0, bi, hh, pl.ds(r2, 2 * CHUNK), :], st.astype(jnp.bfloat16))
                    work.append((d, bi, hh, cc, u, qkkd, gl, o, st, ws))
        for d, bi, hh, cc, u, qkkd, gl, o, st, ws in work:
            r0 = pl.multiple_of(cc * CHUNK, CHUNK)
            r3 = pl.multiple_of(cc * 3 * CHUNK, CHUNK)
            r8 = pl.multiple_of(cc * SUBLANES, SUBLANES)
            v_new = (u[0, bi, hh, pl.ds(r0, CHUNK), :].astype(jnp.float32) - ws[:CHUNK]).astype(jnp.bfloat16)
            rhs = jnp.concatenate([v_new, zeros] if d == 0 else [zeros, v_new], axis=0)
            r = _dot(qkkd[bi, hh, pl.ds(r3, 3 * CHUNK), :], rhs)
            o[bi, hh, pl.ds(r0, CHUNK), :] = (ws[CHUNK:] + r[:CHUNK]).astype(o.dtype)
            decay = gl[0, bi, hh, pl.ds(r8, SUBLANES), :][0:1]
            state_ref[d, bi, hh] = st * decay + r[CHUNK:]
        return carry

    lax.fori_loop(0, nc, body, 0)


def _gdn_scan(wq, u, qkkd, gl, t):
    _, b, nh, s, _ = u.shape
    nblk = s // t
    g8 = t // CHUNK * SUBLANES

    def specs(d):
        blk = (lambda i: i) if d == 0 else (lambda i: nblk - 1 - i)
        idx = lambda i: (d, 0, 0, blk(i), 0)
        idx4 = lambda i: (0, 0, blk(i), 0)
        return [
            pl.BlockSpec((1, b, nh, 2 * t, LANES), idx),
            pl.BlockSpec((1, b, nh, t, LANES), idx),
            pl.BlockSpec((b, nh, 3 * t, LANES), idx4),
            pl.BlockSpec((1, b, nh, g8, LANES), idx),
        ], pl.BlockSpec((b, nh, t, LANES), idx4)

    in_f, out_f = specs(0)
    in_b, out_b = specs(1)
    o_shape = jax.ShapeDtypeStruct((b, nh, s, LANES), jnp.bfloat16)
    return pl.pallas_call(
        functools.partial(_gdn_scan_kernel, t=t),
        grid=(nblk,),
        in_specs=in_f + in_b,
        out_specs=[out_f, out_b],
        out_shape=[o_shape, o_shape],
        scratch_shapes=[pltpu.VMEM((2, b, nh, DN_HEAD_DIM, DN_HEAD_DIM), jnp.float32)],
        compiler_params=_params("arbitrary"),
        name="gdn_scan",
    )(wq, u, qkkd, gl, wq, u, qkkd, gl)


def _merge_kernel(of_ref, ob_ref, z_ref, ga_ref, gb_ref, ya_ref, x_ref, gt_ref, og_ref,
                  wa_ref, wb_ref, wo_ref, o_ref):
    og = og_ref[...]
    z = z_ref[0].astype(jnp.float32)
    parts = []
    for hh in range(DN_HEADS):
        o = of_ref[0, hh].astype(jnp.float32) + ob_ref[0, hh].astype(jnp.float32)
        y = o * lax.rsqrt(jnp.mean(o * o, axis=-1, keepdims=True) + EPS) * og
        parts.append((y * _silu(z[:, hh * LANES:(hh + 1) * LANES])).astype(jnp.bfloat16))
    yb = jnp.concatenate(parts, axis=1)
    pa = _dot(ya_ref[0], wa_ref[...])
    pb = _dot(yb, wb_ref[...])
    merged = (jax.nn.sigmoid(ga_ref[0].astype(jnp.float32)) * pa
              + jax.nn.sigmoid(gb_ref[0].astype(jnp.float32)) * pb)
    o_ref[0] = x_ref[0] + gt_ref[...] * _dot(merged.astype(jnp.bfloat16), wo_ref[...])


def _merge(o_f, o_b, proj, y_a, x, mod, out_gain, w_a, w_b, w_o, layer, tm):
    b, s, d = x.shape
    nh = DN_HEADS
    zw = DN_WIDTH // LANES
    return pl.pallas_call(
        _merge_kernel,
        grid=(b, s // tm),
        in_specs=[
            pl.BlockSpec((1, nh, tm, LANES), lambda bi, i: (bi, 0, i, 0)),
            pl.BlockSpec((1, nh, tm, LANES), lambda bi, i: (bi, 0, i, 0)),
            pl.BlockSpec((1, tm, DN_WIDTH), lambda bi, i: (bi, i, COL_DN_Z // zw)),
            pl.BlockSpec((1, tm, d), lambda bi, i: (bi, i, 0)),
            pl.BlockSpec((1, tm, d), lambda bi, i: (bi, i, 1)),
            pl.BlockSpec((1, tm, DA_WIDTH), lambda bi, i: (bi, i, 0)),
            pl.BlockSpec((1, tm, d), lambda bi, i: (bi, i, 0)),
            _mod_spec(d, layer, MOD_GATE1),
            _layer_spec((1, LANES), layer),
            _layer_spec(w_a.shape[1:], layer),
            _layer_spec(w_b.shape[1:], layer),
            _layer_spec(w_o.shape[1:], layer),
        ],
        out_specs=pl.BlockSpec((1, tm, d), lambda bi, i: (bi, i, 0)),
        out_shape=jax.ShapeDtypeStruct((b, s, d), jnp.float32),
        compiler_params=_params("parallel", "parallel"),
        name="merge",
    )(o_f, o_b, proj, proj, proj, y_a, x, mod, out_gain, w_a, w_b, w_o)


def _mlp_kernel(x_ref, g_ref, sc_ref, sh_ref, gt_ref, w1_ref, w2_ref, o_ref, h_ref, acc_ref):
    j = pl.program_id(2)

    @pl.when(j == 0)
    def _():
        h_ref[...] = _modulated_norm(x_ref[0], g_ref[...], sc_ref[...], sh_ref[...]).astype(h_ref.dtype)
        acc_ref[...] = jnp.zeros_like(acc_ref)

    a = jnp.maximum(_dot(h_ref[...], w1_ref[...]), 0.0)
    acc_ref[...] += _dot((a * a).astype(jnp.bfloat16), w2_ref[...])

    @pl.when(j == pl.num_programs(2) - 1)
    def _():
        o_ref[0] = x_ref[0] + gt_ref[...] * acc_ref[...]


def _mlp(x, gains, mod, w1, w2, layer, tm, tf):
    b, s, d = x.shape
    f = w1.shape[2]
    return pl.pallas_call(
        _mlp_kernel,
        grid=(b, s // tm, f // tf),
        in_specs=[
            pl.BlockSpec((1, tm, d), lambda bi, i, j: (bi, i, 0)),
            _layer_spec((1, d), layer),
            _mod_spec(d, layer, MOD_SCALE2),
            _mod_spec(d, layer, MOD_SHIFT2),
            _mod_spec(d, layer, MOD_GATE2),
            pl.BlockSpec((None, d, tf), lambda bi, i, j: (layer, 0, j)),
            pl.BlockSpec((None, tf, d), lambda bi, i, j: (layer, j, 0)),
        ],
        out_specs=pl.BlockSpec((1, tm, d), lambda bi, i, j: (bi, i, 0)),
        out_shape=jax.ShapeDtypeStruct((b, s, d), jnp.float32),
        scratch_shapes=[pltpu.VMEM((tm, d), jnp.bfloat16), pltpu.VMEM((tm, d), jnp.float32)],
        compiler_params=_params("parallel", "parallel", "arbitrary"),
        name="mlp",
    )(x, gains, mod, mod, mod, w1, w2)


def _reorder_w_in(w_in, d_model):
    assert d_model == D_MODEL
    main = 3 * DA_WIDTH + 4 * DN_WIDTH
    small = 4 * DN_HEADS
    head = w_in[..., :main]
    ba = w_in[..., main:main + small]
    gates = w_in[..., main + small:]
    pad = jnp.zeros(w_in.shape[:-1] + (LANES - small,), w_in.dtype)
    return (jnp.concatenate([gates, head], axis=-1).astype(jnp.bfloat16),
            jnp.concatenate([ba, pad], axis=-1).astype(jnp.bfloat16))


def _rope_tables(seq):
    pos = jnp.arange(seq, dtype=jnp.float32)
    inv = 1.0 / (ROPE_THETA ** (jnp.arange(0, DA_HEAD_DIM, 2, dtype=jnp.float32) / DA_HEAD_DIM))
    ang = pos[:, None] * inv[None, :]
    ang = jnp.concatenate([ang, ang, ang, ang], axis=-1)
    sign = jnp.where((jnp.arange(LANES) % DA_HEAD_DIM) < DA_HEAD_DIM // 2, -1.0, 1.0)
    return jnp.cos(ang), jnp.sin(ang) * sign


class _Tiles(NamedTuple):
    proj_rows: int
    mlp_rows: int
    mlp_cols: int
    prep_rows: int
    gdn_rows: int
    attn_q: int
    attn_k: int


def _tile_sizes(s):
    return _Tiles(proj_rows=min(512, s), mlp_rows=min(1024, s), mlp_cols=1024, prep_rows=min(1024, s),
                  gdn_rows=min(CHUNK * GDN_UNROLL, s), attn_q=min(512, s), attn_k=min(512, s // ATTN_UNROLL))


def kernel(x, c, ada_w, ada_b, norm1_g, norm2_g, w_in, dn_conv, dn_A_log, dn_dt_bias, dn_out_gain,
           qk_norm_gain, diff_lambda, diff_subln_gain, w_branch_a, w_branch_b, w_out, w_mlp1, w_mlp2):
    b, s, d = x.shape
    depth = ada_w.shape[0]
    assert s % 512 == 0 and d % LANES == 0 and b <= SUBLANES
    tiles = _tile_sizes(s)

    c_pad = jnp.zeros((SUBLANES, d), jnp.float32).at[:b].set(c)
    mod = _adaln(c_pad, ada_w, ada_b)[:, :b].reshape(depth, b, 6, 1, d)
    cos, sin_signed = _rope_tables(s)

    bf16 = jnp.bfloat16
    small = 4 * DN_HEADS
    w_in_all, w_ba_all = _reorder_w_in(w_in, d)
    norm1, norm2 = norm1_g.reshape(depth, 1, d), norm2_g.reshape(depth, 1, d)
    qk_gains = jnp.tile(qk_norm_gain, (1, 1, 2)).reshape(depth, 2, 1, LANES)
    subln_gain = diff_subln_gain.reshape(depth, 1, LANES)
    out_gain = dn_out_gain.reshape(depth, 1, LANES)
    conv_w = dn_conv.reshape(depth, CONV_K, 3, DN_WIDTH).transpose(0, 2, 1, 3)
    rows = jnp.zeros((depth, 1, LANES), jnp.float32)
    a_log_rows = rows.at[:, 0, small // 2:small].set(dn_A_log.reshape(depth, -1))
    dt_bias_rows = rows.at[:, 0, small // 2:small].set(dn_dt_bias.reshape(depth, -1))
    w_a, w_b, w_o = w_branch_a.astype(bf16), w_branch_b.astype(bf16), w_out.astype(bf16)
    w1, w2 = w_mlp1.astype(bf16), w_mlp2.astype(bf16)

    for l in range(depth):
        lambda_init = 0.8 - 0.6 * math.exp(-0.3 * l)
        proj, ba = _inproj(x, norm1, mod, w_in_all, w_ba_all, l, tiles.proj_rows)
        qs, kt, v = _attn_prep(proj, cos, sin_signed, qk_gains, l, tiles.prep_rows)
        y_a = _attn(qs, kt, v, diff_lambda, subln_gain, l, lambda_init, tiles.attn_q, tiles.attn_k)
        wq, u, qkkd, gl = _gdn_prep(proj, ba, conv_w, a_log_rows, dt_bias_rows, l, tiles.gdn_rows)
        o_f, o_b = _gdn_scan(wq, u, qkkd, gl, tiles.gdn_rows)
        x = _merge(o_f, o_b, proj, y_a, x, mod, out_gain, w_a, w_b, w_o, l, tiles.proj_rows)
        x = _mlp(x, norm2, mod, w1, w2, l, tiles.mlp_rows, tiles.mlp_cols)
    return x
```

```python
import functools
import math
from typing import NamedTuple

import jax
import jax.numpy as jnp
from jax import lax
from jax.experimental import pallas as pl
from jax.experimental.pallas import tpu as pltpu

DA_HEADS = 4
DA_HEAD_DIM = 64
DA_WIDTH = DA_HEADS * 2 * DA_HEAD_DIM
DN_HEADS = 4
DN_HEAD_DIM = 128
DN_WIDTH = DN_HEADS * DN_HEAD_DIM
CONV_K = 5
CHUNK = 64
ROPE_THETA = 10000.0
EPS = 1e-6
LOG2_E = math.log2(math.e)

LANES = 128
SUBLANES = 8
VMEM_LIMIT_BYTES = 56 * 1024 * 1024

D_MODEL = 1024
COL_GATE = 0
COL_DN_Q = 2 * D_MODEL // LANES
COL_DN_K = COL_DN_Q + DN_WIDTH // LANES
COL_DN_V = COL_DN_K + DN_WIDTH // LANES
COL_DN_Z = COL_DN_V + DN_WIDTH // LANES
COL_STORED = COL_DN_Z + DN_WIDTH // LANES
COL_DA_Q = COL_STORED
COL_DA_K = COL_DA_Q + DA_WIDTH // LANES
COL_DA_V = COL_DA_K + DA_WIDTH // LANES


def _params(*sem):
    return pltpu.CompilerParams(dimension_semantics=sem, vmem_limit_bytes=VMEM_LIMIT_BYTES)


def _layer_spec(tail, layer, **kw):
    zeros = (0,) * len(tail)
    return pl.BlockSpec((None,) + tuple(tail), lambda *_: (layer,) + zeros, **kw)


def _mod_spec(d, layer, k):
    return pl.BlockSpec((None, None, None, 1, d), lambda bi, *_: (layer, bi, k, 0, 0))


MOD_SHIFT1, MOD_SCALE1, MOD_GATE1, MOD_SHIFT2, MOD_SCALE2, MOD_GATE2 = range(6)


def _silu(x):
    return x * jax.nn.sigmoid(x)


def _softplus(x):
    return jnp.maximum(x, 0.0) + jnp.log(1.0 + jnp.exp(-jnp.abs(x)))


def _dot(a, b):
    return jnp.dot(a, b, preferred_element_type=jnp.float32)


def _dot_nt(a, b):
    return lax.dot_general(a, b, (((1,), (1,)), ((), ())), preferred_element_type=jnp.float32)


def _split3(x):
    hi = x.astype(jnp.bfloat16).astype(jnp.float32)
    r1 = x - hi
    mid = r1.astype(jnp.bfloat16).astype(jnp.float32)
    lo = (r1 - mid).astype(jnp.bfloat16).astype(jnp.float32)
    return hi, mid, lo


def _adaln_kernel(c_ref, w_ref, b_ref, o_ref):
    c_act = _silu(c_ref[...])
    o_ref[0] = jnp.dot(c_act, w_ref[0], preferred_element_type=jnp.float32,
                       precision=lax.Precision.HIGHEST) + b_ref[0]


def _adaln(c_pad, ada_w, ada_b):
    depth, d, n = ada_w.shape
    tn = 1536
    return pl.pallas_call(
        _adaln_kernel,
        grid=(depth, n // tn),
        in_specs=[
            pl.BlockSpec((SUBLANES, d), lambda l, j: (0, 0)),
            pl.BlockSpec((1, d, tn), lambda l, j: (l, 0, j)),
            pl.BlockSpec((1, 1, tn), lambda l, j: (l, 0, j)),
        ],
        out_specs=pl.BlockSpec((1, SUBLANES, tn), lambda l, j: (l, 0, j)),
        out_shape=jax.ShapeDtypeStruct((depth, SUBLANES, n), jnp.float32),
        compiler_params=_params("parallel", "parallel"),
        name="adaln",
    )(c_pad, ada_w, ada_b.reshape(depth, 1, n))


def _modulated_norm(x, gain, scale, shift):
    y = x * lax.rsqrt(jnp.mean(x * x, axis=-1, keepdims=True) + EPS)
    return (y * gain) * (1.0 + scale) + shift


INPROJ_COLS = 512


def _norm_rope(x, gain, cos, sin):
    lane = lax.broadcasted_iota(jnp.int32, (1, LANES), 1)
    lane_col = lax.broadcasted_iota(jnp.int32, (LANES, 1), 0)
    same_map = ((lane_col < DA_HEAD_DIM) == (lane < DA_HEAD_DIM)).astype(jnp.bfloat16)
    first_half = (lane % DA_HEAD_DIM) < (DA_HEAD_DIM // 2)
    sq = x * x
    hi = sq.astype(jnp.bfloat16)
    lo = (sq - hi.astype(jnp.float32)).astype(jnp.bfloat16)
    ms = (_dot(hi, same_map) + _dot(lo, same_map)) * (1.0 / DA_HEAD_DIM)
    y = x * lax.rsqrt(ms + EPS) * gain
    fwd = pltpu.roll(y, LANES - DA_HEAD_DIM // 2, axis=1)
    bwd = pltpu.roll(y, DA_HEAD_DIM // 2, axis=1)
    return y * cos + jnp.where(first_half, fwd, bwd) * sin


def _inproj_kernel(x_ref, g_ref, sc_ref, sh_ref, w_ref, wba_ref, cos_ref, sin_ref, qg_ref, kg_ref,
                   o_ref, ba_ref, qo_ref, kt_ref, vo_ref):
    h = _modulated_norm(x_ref[0], g_ref[...], sc_ref[...], sh_ref[...]).astype(jnp.bfloat16)

    def cols(base):
        return _dot(h, w_ref[:, base * LANES:base * LANES + DA_WIDTH])

    q_all, k_all, v_all = cols(COL_DA_Q), cols(COL_DA_K), cols(COL_DA_V)
    cos, sin = cos_ref[...], sin_ref[...]
    left = lax.broadcasted_iota(jnp.int32, (1, LANES), 1) < DA_HEAD_DIM

    def q_piece(hh, sl):
        q = _norm_rope(q_all[:, sl], qg_ref[...], cos, sin) * (DA_HEAD_DIM ** -0.5 * LOG2_E)
        qo_ref[0, hh, 0] = jnp.where(left, q, 0.0).astype(qo_ref.dtype)
        qo_ref[0, hh, 1] = jnp.where(left, 0.0, q).astype(qo_ref.dtype)

    def k_piece(hh, sl):
        kt_ref[0, hh] = _norm_rope(k_all[:, sl], kg_ref[...], cos, sin).T.astype(kt_ref.dtype)

    def v_piece(hh, sl):
        v = v_all[:, sl].astype(vo_ref.dtype)
        vo_ref[0, hh] = jnp.concatenate([v, jnp.ones_like(v)], axis=1)

    pieces = [functools.partial(f, hh, slice(hh * LANES, (hh + 1) * LANES))
              for hh in range(DA_HEADS) for f in (q_piece, k_piece, v_piece)]
    starts = list(range(0, COL_STORED * LANES, INPROJ_COLS))
    for n, c in enumerate(starts):
        o_ref[0, :, c:c + INPROJ_COLS] = _dot(h, w_ref[:, c:c + INPROJ_COLS]).astype(o_ref.dtype)
        for piece in pieces[n * len(pieces) // len(starts):(n + 1) * len(pieces) // len(starts)]:
            piece()
    ba_ref[0] = _dot(h, wba_ref[...])


def _inproj(x, gains, mod, w, w_ba, cos, sin_signed, qk_gains, layer, tm):
    b, s, d = x.shape
    n = w.shape[2]
    nh = DA_HEADS
    n_store = COL_STORED * LANES
    assert n_store % INPROJ_COLS == 0 and n == n_store + 3 * DA_WIDTH
    resident = dict(pipeline_mode=pl.Buffered(1))
    bf16 = jnp.bfloat16
    return pl.pallas_call(
        _inproj_kernel,
        grid=(b, s // tm),
        in_specs=[
            pl.BlockSpec((1, tm, d), lambda bi, i: (bi, i, 0)),
            _layer_spec((1, d), layer),
            _mod_spec(d, layer, MOD_SCALE1),
            _mod_spec(d, layer, MOD_SHIFT1),
            _layer_spec((d, n), layer, **resident),
            _layer_spec((d, LANES), layer, **resident),
            pl.BlockSpec((tm, LANES), lambda bi, i: (i, 0)),
            pl.BlockSpec((tm, LANES), lambda bi, i: (i, 0)),
            pl.BlockSpec((None, None, 1, LANES), lambda bi, i: (layer, 0, 0, 0)),
            pl.BlockSpec((None, None, 1, LANES), lambda bi, i: (layer, 1, 0, 0)),
        ],
        out_specs=[pl.BlockSpec((1, tm, n_store), lambda bi, i: (bi, i, 0)),
                   pl.BlockSpec((1, tm, LANES), lambda bi, i: (bi, i, 0)),
                   pl.BlockSpec((1, nh, 2, tm, LANES), lambda bi, i: (bi, 0, 0, i, 0)),
                   pl.BlockSpec((1, nh, LANES, tm), lambda bi, i: (bi, 0, 0, i)),
                   pl.BlockSpec((1, nh, tm, 2 * LANES), lambda bi, i: (bi, 0, i, 0))],
        out_shape=[jax.ShapeDtypeStruct((b, s, n_store), bf16),
                   jax.ShapeDtypeStruct((b, s, LANES), jnp.float32),
                   jax.ShapeDtypeStruct((b, nh, 2, s, LANES), bf16),
                   jax.ShapeDtypeStruct((b, nh, LANES, s), bf16),
                   jax.ShapeDtypeStruct((b, nh, s, 2 * LANES), bf16)],
        compiler_params=_params("parallel", "parallel"),
        name="inproj",
    )(x, gains, mod, mod, w, w_ba, cos, sin_signed, qk_gains, qk_gains)


ATTN_UNROLL = 8
ATTN_ROW_BLOCK = 32


def _attn_kernel(q_ref, kt_ref, v_ref, lam_ref, g_ref, o_ref, s_ref, p_ref, m_ref, alpha_ref, acc_ref,
                 *, tq, tk, lambda_init):
    s_len = v_ref.shape[2]
    q = q_ref[0, 0].reshape(2 * tq, LANES)
    n_kv = s_len // tk
    groups = tk // LANES
    rb = ATTN_ROW_BLOCK

    m_ref[...] = jnp.full(m_ref.shape, -jnp.inf, jnp.float32)
    acc_ref[...] = jnp.zeros(acc_ref.shape, jnp.float32)

    def scores(j, slot):
        s_ref[slot] = _dot(q, kt_ref[0, 0, :, pl.ds(pl.multiple_of(j * tk, tk), tk)])

    def update(j, slot):
        for r in range(2 * tq // rb):
            rows = slice(r * rb, (r + 1) * rb)
            sg = [s_ref[slot, rows, g * LANES:(g + 1) * LANES] for g in range(groups)]
            m_old = m_ref[rows, :]
            m_new = jnp.maximum(m_old, jnp.max(functools.reduce(jnp.maximum, sg), axis=-1, keepdims=True))
            alpha_ref[rows, :] = jnp.exp2(m_old - m_new)
            m_ref[rows, :] = m_new
            for g in range(groups):
                p_ref[rows, g * LANES:(g + 1) * LANES] = jnp.exp2(sg[g] - m_new).astype(jnp.bfloat16)
        alpha = alpha_ref[...]
        pv = _dot(p_ref[...], v_ref[0, 0, pl.ds(pl.multiple_of(j * tk, tk), tk), :])
        acc_ref[...] = jnp.concatenate([alpha, alpha], axis=1) * acc_ref[...] + pv

    def body(jj, carry):
        j = ATTN_UNROLL * jj
        for k in range(ATTN_UNROLL):
            scores(jnp.minimum(j + k + 1, n_kv - 1), (k + 1) % 2)
            update(j + k, k % 2)
        return carry

    scores(0, 0)
    lax.fori_loop(0, n_kv // ATTN_UNROLL, body, 0)
    o = acc_ref[:, :LANES] / acc_ref[:, LANES:]
    lq = lam_ref[...]
    lam = (jnp.exp(jnp.sum(lq[0:1] * lq[1:2], axis=-1, keepdims=True))
           - jnp.exp(jnp.sum(lq[2:3] * lq[3:4], axis=-1, keepdims=True)) + lambda_init)
    d = o[:tq] - lam * o[tq:]
    y = d * lax.rsqrt(jnp.mean(d * d, axis=-1, keepdims=True) + EPS) * g_ref[...]
    o_ref[0] = (y * (1.0 - lambda_init)).astype(o_ref.dtype)


def _attn(qs, kt, v, lam_params, subln_gain, layer, lambda_init, tq, tk):
    b, h, _, s, _ = qs.shape
    assert s % (tk * ATTN_UNROLL) == 0 and ATTN_UNROLL % 2 == 0 and (2 * tq) % ATTN_ROW_BLOCK == 0
    return pl.pallas_call(
        functools.partial(_attn_kernel, tq=tq, tk=tk, lambda_init=lambda_init),
        grid=(b, h, s // tq),
        in_specs=[
            pl.BlockSpec((1, 1, 2, tq, LANES), lambda bi, hi, i: (bi, hi, 0, i, 0)),
            pl.BlockSpec((1, 1, LANES, s), lambda bi, hi, i: (bi, hi, 0, 0)),
            pl.BlockSpec((1, 1, s, 2 * LANES), lambda bi, hi, i: (bi, hi, 0, 0)),
            _layer_spec(lam_params.shape[1:], layer),
            _layer_spec((1, LANES), layer),
        ],
        out_specs=pl.BlockSpec((1, tq, LANES), lambda bi, hi, i: (bi, i, hi)),
        out_shape=jax.ShapeDtypeStruct((b, s, h * LANES), jnp.bfloat16),
        scratch_shapes=[pltpu.VMEM((2, 2 * tq, tk), jnp.float32), pltpu.VMEM((2 * tq, tk), jnp.bfloat16),
                        pltpu.VMEM((2 * tq, LANES), jnp.float32), pltpu.VMEM((2 * tq, LANES), jnp.float32),
                        pltpu.VMEM((2 * tq, 2 * LANES), jnp.float32)],
        compiler_params=_params("parallel", "parallel", "parallel"),
        name="attn",
    )(qs, kt, v, lam_params, subln_gain)


INV_BLOCK = 16
HALO = 16
GDN_UNROLL = 8


def _gdn_prep_kernel(q_ref, qp_ref, qn_ref, k_ref, kp_ref, kn_ref, v_ref, vp_ref, vn_ref,
                     cw_ref, ba_ref, alog_ref, dtb_ref,
                     wq_ref, u_ref, qkkd_ref, gl_ref,
                     qs_ref, ks_ref, vs_ref, bf_ref, bb_ref, gf_ref, gb_ref, xq_ref, xk_ref, xv_ref,
                     *, t, nblk):
    i = pl.program_id(1)
    pad = CONV_K // 2
    bf = jnp.bfloat16

    def conv_silu(h, x_ref, p_ref, n_ref, w, xp_ref):
        f32 = jnp.float32
        cols = slice(h * LANES, (h + 1) * LANES)
        xp_ref[h, 0:HALO, :] = jnp.where(i == 0, 0.0, p_ref[0, :, cols].astype(f32))
        xp_ref[h, HALO:HALO + t, :] = x_ref[0, :, cols].astype(f32)
        xp_ref[h, HALO + t:2 * HALO + t, :] = jnp.where(i == nblk - 1, 0.0, n_ref[0, :, cols].astype(f32))
        acc = xp_ref[h, HALO - pad:HALO - pad + t, :] * w[0:1, cols]
        for j in range(1, CONV_K):
            acc = acc + xp_ref[h, HALO - pad + j:HALO - pad + j + t, :] * w[j:j + 1, cols]
        return _silu(acc)

    def l2n(x):
        return x * lax.rsqrt(jnp.sum(x * x, axis=-1, keepdims=True) + EPS)

    cw = cw_ref[...]
    lane = lax.broadcasted_iota(jnp.int32, (1, LANES), 1)

    r3 = lax.broadcasted_iota(jnp.int32, (CHUNK, 3 * CHUNK), 0)
    c3 = lax.broadcasted_iota(jnp.int32, (CHUNK, 3 * CHUNK), 1) % CHUNK
    tril3 = (r3 >= c3).astype(bf)
    triu3 = (r3 <= c3).astype(bf)

    ba = ba_ref[0]
    g_all = -jnp.exp(alog_ref[...]) * _softplus(ba + dtb_ref[...])
    beta_all = jax.nn.sigmoid(ba)
    hi, mid, lo = _split3(g_all)
    gcf, gcb = [], []
    for c in range(t // CHUNK):
        rows = slice(c * CHUNK, (c + 1) * CHUNK)
        parts = jnp.concatenate([hi[rows], mid[rows], lo[rows]], axis=0).astype(bf)
        gcf.append(_dot(tril3, parts))
        gcb.append(_dot(triu3, parts))

    def pick(x, idx):
        col = jnp.sum(jnp.where(lane == idx, x, 0.0), axis=-1, keepdims=True)
        return jnp.broadcast_to(col, (t, LANES))

    gcf = jnp.concatenate(gcf, axis=0)
    gcb = jnp.concatenate(gcb, axis=0)

    row = lax.broadcasted_iota(jnp.int32, (CHUNK, 2 * CHUNK), 0)
    col2 = lax.broadcasted_iota(jnp.int32, (CHUNK, 2 * CHUNK), 1)
    left = col2 < CHUNK
    colm = col2 % CHUNK
    keep2 = (left & (row >= colm)) | (~left & (row <= colm))
    strict2 = keep2 & (row != colm)
    diag2 = row == colm
    eye2 = diag2.astype(jnp.float32)
    same2 = (row // INV_BLOCK) == (colm // INV_BLOCK)
    row4 = lax.broadcasted_iota(jnp.int32, (2 * CHUNK, 2 * CHUNK), 0)
    col4 = lax.broadcasted_iota(jnp.int32, (2 * CHUNK, 2 * CHUNK), 1)
    bd_mask = ((row4 < CHUNK) == (col4 < CHUNK)).astype(bf)

    def blockdiag(y):
        yb = y.astype(bf)
        return jnp.concatenate([yb, yb], axis=0) * bd_mask

    def st_load(h, c):
        rows = pl.ds(pl.multiple_of(c * CHUNK, CHUNK), CHUNK)
        v = dict(h=h, c=c, rows=rows, qc=qs_ref[h, rows, :], kc=ks_ref[h, rows, :], vc=vs_ref[h, rows, :],
                 beta_f=bf_ref[h, rows, :], beta_b=bb_ref[h, rows, :],
                 gc_f=gf_ref[h, rows, :], gc_b=gb_ref[h, rows, :])
        kcb = v['kc'].astype(bf)
        kk = jnp.concatenate([kcb, kcb], axis=0)
        v['gram2'] = _dot_nt(kcb, kk)
        v['qk2'] = _dot_nt(v['qc'].astype(bf), kk)
        return v

    def st_lmat(v):
        gc2 = jnp.where(left, v['gc_f'], v['gc_b'])
        gc_row = jnp.sum(jnp.where(diag2, gc2, 0.0), axis=0, keepdims=True)
        decay2 = jnp.where(keep2, jnp.exp(jnp.where(keep2, gc2 - gc_row, 0.0)), 0.0)
        l2 = jnp.where(strict2, jnp.where(left, v['beta_f'], v['beta_b']) * v['gram2'] * decay2, 0.0)
        ld = jnp.where(same2, l2, 0.0)
        v.update(qkm=(v['qk2'] * decay2).astype(bf), lo=l2 - ld, p=eye2 - ld,
                 m=_dot(ld.astype(bf), blockdiag(ld)))
        del v['gram2'], v['qk2']
        return v

    def st_level(v):
        r = _dot(jnp.concatenate([v['p'], v['m']], axis=0).astype(bf), blockdiag(v['m']))
        v.update(p=v['p'] + r[:CHUNK], m=r[CHUNK:])
        return v

    def st_dinv(v):
        v['d_inv'] = v['p'] + _dot(v['p'].astype(bf), blockdiag(v['m']))
        return v

    def st_n(v):
        v['n'] = _dot(v['d_inv'].astype(bf), blockdiag(v['lo']))
        return v

    def st_nsq(v):
        v['nsq'] = _dot(v['n'].astype(bf), blockdiag(v['n']))
        return v

    def st_pn(v):
        pn = eye2 - v['n']
        v['pn'] = pn + _dot(pn.astype(bf), blockdiag(v['nsq']))
        return v

    def st_t(v):
        v['t2'] = _dot(v['pn'].astype(bf), blockdiag(v['d_inv']))
        return v

    def st_uw(v):
        kc, vc = v['kc'], v['vc']
        v['egc_f'] = jnp.exp(v['gc_f'])
        v['egc_b'] = jnp.exp(v['gc_b'])
        rhs = jnp.concatenate([
            jnp.concatenate([vc * v['beta_f'], kc * (v['beta_f'] * v['egc_f'])], axis=1),
            jnp.concatenate([vc * v['beta_b'], kc * (v['beta_b'] * v['egc_b'])], axis=1)], axis=0).astype(bf)
        v['uw'] = _dot(blockdiag(v['t2']), rhs)
        return v

    def st_store(v):
        h, c, rows, uw, qc, kc = v['h'], v['c'], v['rows'], v['uw'], v['qc'], v['kc']
        r2 = pl.multiple_of(c * 2 * CHUNK, 2 * CHUNK)
        for d, egc in ((0, v['egc_f']), (1, v['egc_b'])):
            u_ref[d, 0, h, rows, :] = uw[d * CHUNK:(d + 1) * CHUNK, :LANES].astype(bf)
            wq_ref[d, 0, h, pl.ds(r2, CHUNK), :] = uw[d * CHUNK:(d + 1) * CHUNK, LANES:].astype(bf)
            wq_ref[d, 0, h, pl.ds(r2 + CHUNK, CHUNK), :] = (qc * egc).astype(bf)
        gtot_f = v['gc_f'][CHUNK - 1:CHUNK, :]
        gtot_b = v['gc_b'][0:1, :]
        kd = jnp.concatenate([kc * jnp.exp(gtot_f - v['gc_f']), kc * jnp.exp(gtot_b - v['gc_b'])], axis=0)
        r3_ = pl.multiple_of(c * 3 * CHUNK, CHUNK)
        qkkd_ref[0, h, pl.ds(r3_, CHUNK), :] = v['qkm']
        qkkd_ref[0, h, pl.ds(r3_ + CHUNK, 2 * CHUNK), :] = kd.T.astype(bf)
        r8 = pl.multiple_of(c * SUBLANES, SUBLANES)
        gl_ref[0, 0, h, pl.ds(r8, SUBLANES), :] = jnp.broadcast_to(jnp.exp(gtot_f), (SUBLANES, LANES))
        gl_ref[1, 0, h, pl.ds(r8, SUBLANES), :] = jnp.broadcast_to(jnp.exp(gtot_b), (SUBLANES, LANES))

    levels = [st_level] * (int(math.log2(INV_BLOCK)) - 2)
    stages = [st_lmat] + levels + [st_dinv, st_n, st_nsq, st_pn, st_t, st_uw]

    for h in range(DN_HEADS):
        qs_ref[h] = l2n(conv_silu(h, q_ref, qp_ref, qn_ref, cw[0], xq_ref)) * (DN_HEAD_DIM ** -0.5)
        ks_ref[h] = l2n(conv_silu(h, k_ref, kp_ref, kn_ref, cw[1], xk_ref))
        vs_ref[h] = conv_silu(h, v_ref, vp_ref, vn_ref, cw[2], xv_ref)
        bf_ref[h] = pick(beta_all, h)
        bb_ref[h] = pick(beta_all, DN_HEADS + h)
        gf_ref[h] = pick(gcf, 2 * DN_HEADS + h)
        gb_ref[h] = pick(gcb, 3 * DN_HEADS + h)

        def trip(j, carry, h=h):
            vals = [st_load(h, j * GDN_UNROLL + k) for k in range(GDN_UNROLL)]
            for stage in stages:
                vals = [stage(v) for v in vals]
            for v in vals:
                st_store(v)
            return carry

        lax.fori_loop(0, t // (CHUNK * GDN_UNROLL), trip, 0)


def _gdn_prep(proj, ba, conv_w, a_log_row, dt_bias_row, layer, t):
    b, s, _ = proj.shape
    nh = DN_HEADS
    nblk = s // t
    n = s // CHUNK
    tpb = t // HALO
    wcol = DN_WIDTH // LANES

    def main(base):
        return pl.BlockSpec((1, t, DN_WIDTH), lambda bi, i: (bi, i, base // wcol))

    def prev(base):
        return pl.BlockSpec((1, HALO, DN_WIDTH),
                            lambda bi, i: (bi, jnp.maximum(i * tpb - 1, 0), base // wcol))

    def nxt(base):
        return pl.BlockSpec((1, HALO, DN_WIDTH),
                            lambda bi, i: (bi, jnp.minimum((i + 1) * tpb, s // HALO - 1), base // wcol))

    in_specs = []
    for base in (COL_DN_Q, COL_DN_K, COL_DN_V):
        in_specs += [main(base), prev(base), nxt(base)]
    in_specs += [
        _layer_spec((3, CONV_K, DN_WIDTH), layer),
        pl.BlockSpec((1, t, LANES), lambda bi, i: (bi, i, 0)),
        _layer_spec((1, LANES), layer),
        _layer_spec((1, LANES), layer),
    ]
    out_specs = [
        pl.BlockSpec((2, 1, nh, 2 * t, LANES), lambda bi, i: (0, bi, 0, i, 0)),
        pl.BlockSpec((2, 1, nh, t, LANES), lambda bi, i: (0, bi, 0, i, 0)),
        pl.BlockSpec((1, nh, 3 * t, LANES), lambda bi, i: (bi, 0, i, 0)),
        pl.BlockSpec((2, 1, nh, t // CHUNK * SUBLANES, LANES), lambda bi, i: (0, bi, 0, i, 0)),
    ]
    out_shape = [
        jax.ShapeDtypeStruct((2, b, nh, 2 * s, LANES), jnp.bfloat16),
        jax.ShapeDtypeStruct((2, b, nh, s, LANES), jnp.bfloat16),
        jax.ShapeDtypeStruct((b, nh, 3 * s, LANES), jnp.bfloat16),
        jax.ShapeDtypeStruct((2, b, nh, n * SUBLANES, LANES), jnp.float32),
    ]
    proj9 = [proj] * 9
    return pl.pallas_call(
        functools.partial(_gdn_prep_kernel, t=t, nblk=nblk),
        grid=(b, nblk),
        in_specs=in_specs,
        out_specs=out_specs,
        out_shape=out_shape,
        scratch_shapes=([pltpu.VMEM((nh, t, LANES), jnp.float32)] * 7
                        + [pltpu.VMEM((nh, t + 2 * HALO, LANES), jnp.float32)] * 3),
        compiler_params=_params("parallel", "parallel"),
        name="gdn_prep",
    )(*proj9, conv_w, ba, a_log_row, dt_bias_row)


def _gdn_scan_kernel(wq_f, u_f, qkkd_f, gl_f, wq_b, u_b, qkkd_b, gl_b, o_f, o_b, state_ref, *, t):
    nc = t // CHUNK
    nb = u_f.shape[1]

    @pl.when(pl.program_id(0) == 0)
    def _():
        state_ref[...] = jnp.zeros_like(state_ref)

    streams = ((0, wq_f, u_f, qkkd_f, gl_f, o_f), (1, wq_b, u_b, qkkd_b, gl_b, o_b))
    zeros = jnp.zeros((CHUNK, DN_HEAD_DIM), jnp.bfloat16)

    def body(c, carry):
        work = []
        for d, wq, u, qkkd, gl, o in streams:
            cc = c if d == 0 else nc - 1 - c
            r2 = pl.multiple_of(cc * 2 * CHUNK, 2 * CHUNK)
            for bi in range(nb):
                for hh in range(DN_HEADS):
                    st = state_ref[d, bi, hh]
                    ws = _dot(wq[0, bi, hh, pl.ds(r2, 2 * CHUNK), :], st.astype(jnp.bfloat16))
                    work.append((d, bi, hh, cc, u, qkkd, gl, o, st, ws))
        for d, bi, hh, cc, u, qkkd, gl, o, st, ws in work:
            r0 = pl.multiple_of(cc * CHUNK, CHUNK)
            r3 = pl.multiple_of(cc * 3 * CHUNK, CHUNK)
            r8 = pl.multiple_of(cc * SUBLANES, SUBLANES)
            v_new = (u[0, bi, hh, pl.ds(r0, CHUNK), :].astype(jnp.float32) - ws[:CHUNK]).astype(jnp.bfloat16)
            rhs = jnp.concatenate([v_new, zeros] if d == 0 else [zeros, v_new], axis=0)
            r = _dot(qkkd[bi, hh, pl.ds(r3, 3 * CHUNK), :], rhs)
            o[bi, hh, pl.ds(r0, CHUNK), :] = (ws[CHUNK:] + r[:CHUNK]).astype(o.dtype)
            decay = gl[0, bi, hh, pl.ds(r8, SUBLANES), :][0:1]
            state_ref[d, bi, hh] = st * decay + r[CHUNK:]
        return carry

    lax.fori_loop(0, nc, body, 0)


def _gdn_scan(wq, u, qkkd, gl, t):
    _, b, nh, s, _ = u.shape
    nblk = s // t
    g8 = t // CHUNK * SUBLANES

    def specs(d):
        blk = (lambda i: i) if d == 0 else (lambda i: nblk - 1 - i)
        idx = lambda i: (d, 0, 0, blk(i), 0)
        idx4 = lambda i: (0, 0, blk(i), 0)
        return [
            pl.BlockSpec((1, b, nh, 2 * t, LANES), idx),
            pl.BlockSpec((1, b, nh, t, LANES), idx),
            pl.BlockSpec((b, nh, 3 * t, LANES), idx4),
            pl.BlockSpec((1, b, nh, g8, LANES), idx),
        ], pl.BlockSpec((b, nh, t, LANES), idx4)

    in_f, out_f = specs(0)
    in_b, out_b = specs(1)
    o_shape = jax.ShapeDtypeStruct((b, nh, s, LANES), jnp.bfloat16)
    return pl.pallas_call(
        functools.partial(_gdn_scan_kernel, t=t),
        grid=(nblk,),
        in_specs=in_f + in_b,
        out_specs=[out_f, out_b],
        out_shape=[o_shape, o_shape],
        scratch_shapes=[pltpu.VMEM((2, b, nh, DN_HEAD_DIM, DN_HEAD_DIM), jnp.float32)],
        compiler_params=_params("arbitrary"),
        name="gdn_scan",
    )(wq, u, qkkd, gl, wq, u, qkkd, gl)


def _merge_kernel(of_ref, ob_ref, z_ref, ga_ref, gb_ref, ya_ref, x_ref, gt_ref, og_ref,
                  wa_ref, wb_ref, wo_ref, o_ref):
    og = og_ref[...]
    z = z_ref[0].astype(jnp.float32)
    parts = []
    for hh in range(DN_HEADS):
        o = of_ref[0, hh].astype(jnp.float32) + ob_ref[0, hh].astype(jnp.float32)
        y = o * lax.rsqrt(jnp.mean(o * o, axis=-1, keepdims=True) + EPS) * og
        parts.append((y * _silu(z[:, hh * LANES:(hh + 1) * LANES])).astype(jnp.bfloat16))
    yb = jnp.concatenate(parts, axis=1)
    pa = _dot(ya_ref[0], wa_ref[...])
    pb = _dot(yb, wb_ref[...])
    merged = (jax.nn.sigmoid(ga_ref[0].astype(jnp.float32)) * pa
              + jax.nn.sigmoid(gb_ref[0].astype(jnp.float32)) * pb)
    o_ref[0] = x_ref[0] + gt_ref[...] * _dot(merged.astype(jnp.bfloat16), wo_ref[...])


def _merge(o_f, o_b, proj, y_a, x, mod, out_gain, w_a, w_b, w_o, layer, tm):
    b, s, d = x.shape
    nh = DN_HEADS
    zw = DN_WIDTH // LANES
    return pl.pallas_call(
        _merge_kernel,
        grid=(b, s // tm),
        in_specs=[
            pl.BlockSpec((1, nh, tm, LANES), lambda bi, i: (bi, 0, i, 0)),
            pl.BlockSpec((1, nh, tm, LANES), lambda bi, i: (bi, 0, i, 0)),
            pl.BlockSpec((1, tm, DN_WIDTH), lambda bi, i: (bi, i, COL_DN_Z // zw)),
            pl.BlockSpec((1, tm, d), lambda bi, i: (bi, i, 0)),
            pl.BlockSpec((1, tm, d), lambda bi, i: (bi, i, 1)),
            pl.BlockSpec((1, tm, DA_WIDTH), lambda bi, i: (bi, i, 0)),
            pl.BlockSpec((1, tm, d), lambda bi, i: (bi, i, 0)),
            _mod_spec(d, layer, MOD_GATE1),
            _layer_spec((1, LANES), layer),
            _layer_spec(w_a.shape[1:], layer),
            _layer_spec(w_b.shape[1:], layer),
            _layer_spec(w_o.shape[1:], layer),
        ],
        out_specs=pl.BlockSpec((1, tm, d), lambda bi, i: (bi, i, 0)),
        out_shape=jax.ShapeDtypeStruct((b, s, d), jnp.float32),
        compiler_params=_params("parallel", "parallel"),
        name="merge",
    )(o_f, o_b, proj, proj, proj, y_a, x, mod, out_gain, w_a, w_b, w_o)


def _mlp_kernel(x_ref, g_ref, sc_ref, sh_ref, gt_ref, w1_ref, w2_ref, o_ref, h_ref, acc_ref):
    j = pl.program_id(2)

    @pl.when(j == 0)
    def _():
        h_ref[...] = _modulated_norm(x_ref[0], g_ref[...], sc_ref[...], sh_ref[...]).astype(h_ref.dtype)
        acc_ref[...] = jnp.zeros_like(acc_ref)

    a = jnp.maximum(_dot(h_ref[...], w1_ref[...]), 0.0)
    acc_ref[...] += _dot((a * a).astype(jnp.bfloat16), w2_ref[...])

    @pl.when(j == pl.num_programs(2) - 1)
    def _():
        o_ref[0] = x_ref[0] + gt_ref[...] * acc_ref[...]


def _mlp(x, gains, mod, w1, w2, layer, tm, tf):
    b, s, d = x.shape
    f = w1.shape[2]
    return pl.pallas_call(
        _mlp_kernel,
        grid=(b, s // tm, f // tf),
        in_specs=[
            pl.BlockSpec((1, tm, d), lambda bi, i, j: (bi, i, 0)),
            _layer_spec((1, d), layer),
            _mod_spec(d, layer, MOD_SCALE2),
            _mod_spec(d, layer, MOD_SHIFT2),
            _mod_spec(d, layer, MOD_GATE2),
            pl.BlockSpec((None, d, tf), lambda bi, i, j: (layer, 0, j)),
            pl.BlockSpec((None, tf, d), lambda bi, i, j: (layer, j, 0)),
        ],
        out_specs=pl.BlockSpec((1, tm, d), lambda bi, i, j: (bi, i, 0)),
        out_shape=jax.ShapeDtypeStruct((b, s, d), jnp.float32),
        scratch_shapes=[pltpu.VMEM((tm, d), jnp.bfloat16), pltpu.VMEM((tm, d), jnp.float32)],
        compiler_params=_params("parallel", "parallel", "arbitrary"),
        name="mlp",
    )(x, gains, mod, mod, mod, w1, w2)


def _reorder_w_in(w_in, d_model):
    assert d_model == D_MODEL
    w16 = w_in.astype(jnp.bfloat16)
    da, main = 3 * DA_WIDTH, 3 * DA_WIDTH + 4 * DN_WIDTH
    small = 4 * DN_HEADS
    pad = jnp.zeros(w16.shape[:-1] + (LANES - small,), w16.dtype)
    return (jnp.concatenate([w16[..., main + small:], w16[..., da:main], w16[..., :da]], axis=-1),
            jnp.concatenate([w16[..., main:main + small], pad], axis=-1))


def _rope_tables(seq):
    pos = jnp.arange(seq, dtype=jnp.float32)
    inv = 1.0 / (ROPE_THETA ** (jnp.arange(0, DA_HEAD_DIM, 2, dtype=jnp.float32) / DA_HEAD_DIM))
    ang = pos[:, None] * inv[None, :]
    ang = jnp.concatenate([ang, ang, ang, ang], axis=-1)
    sign = jnp.where((jnp.arange(LANES) % DA_HEAD_DIM) < DA_HEAD_DIM // 2, -1.0, 1.0)
    return jnp.cos(ang), jnp.sin(ang) * sign


class _Tiles(NamedTuple):
    proj_rows: int
    mlp_rows: int
    mlp_cols: int
    gdn_rows: int
    attn_q: int
    attn_k: int


def _tile_sizes(s):
    return _Tiles(proj_rows=min(512, s), mlp_rows=min(1024, s), mlp_cols=1024,
                  gdn_rows=min(CHUNK * GDN_UNROLL, s), attn_q=min(512, s), attn_k=min(512, s // ATTN_UNROLL))


def kernel(x, c, ada_w, ada_b, norm1_g, norm2_g, w_in, dn_conv, dn_A_log, dn_dt_bias, dn_out_gain,
           qk_norm_gain, diff_lambda, diff_subln_gain, w_branch_a, w_branch_b, w_out, w_mlp1, w_mlp2):
    b, s, d = x.shape
    depth = ada_w.shape[0]
    assert s % 512 == 0 and d % LANES == 0 and b <= SUBLANES
    tiles = _tile_sizes(s)

    c_pad = jnp.zeros((SUBLANES, d), jnp.float32).at[:b].set(c)
    mod = _adaln(c_pad, ada_w, ada_b)[:, :b].reshape(depth, b, 6, 1, d)
    cos, sin_signed = _rope_tables(s)

    bf16 = jnp.bfloat16
    small = 4 * DN_HEADS
    w_in_all, w_ba_all = _reorder_w_in(w_in, d)
    norm1, norm2 = norm1_g.reshape(depth, 1, d), norm2_g.reshape(depth, 1, d)
    qk_gains = jnp.tile(qk_norm_gain, (1, 1, 2)).reshape(depth, 2, 1, LANES)
    subln_gain = diff_subln_gain.reshape(depth, 1, LANES)
    out_gain = dn_out_gain.reshape(depth, 1, LANES)
    conv_w = dn_conv.reshape(depth, CONV_K, 3, DN_WIDTH).transpose(0, 2, 1, 3)
    rows = jnp.zeros((depth, 1, LANES), jnp.float32)
    a_log_rows = rows.at[:, 0, small // 2:small].set(dn_A_log.reshape(depth, -1))
    dt_bias_rows = rows.at[:, 0, small // 2:small].set(dn_dt_bias.reshape(depth, -1))
    w_a, w_b, w_o = w_branch_a.astype(bf16), w_branch_b.astype(bf16), w_out.astype(bf16)
    w1, w2 = w_mlp1.astype(bf16), w_mlp2.astype(bf16)

    for l in range(depth):
        lambda_init = 0.8 - 0.6 * math.exp(-0.3 * l)
        proj, ba, qs, kt, v = _inproj(x, norm1, mod, w_in_all, w_ba_all, cos, sin_signed, qk_gains, l,
                                      tiles.proj_rows)
        y_a = _attn(qs, kt, v, diff_lambda, subln_gain, l, lambda_init, tiles.attn_q, tiles.attn_k)
        wq, u, qkkd, gl = _gdn_prep(proj, ba, conv_w, a_log_rows, dt_bias_rows, l, tiles.gdn_rows)
        o_f, o_b = _gdn_scan(wq, u, qkkd, gl, tiles.gdn_rows)
        x = _merge(o_f, o_b, proj, y_a, x, mod, out_gain, w_a, w_b, w_o, l, tiles.proj_rows)
        x = _mlp(x, norm2, mod, w1, w2, l, tiles.mlp_rows, tiles.mlp_cols)
    return x
```

```python
import functools
import math
from typing import NamedTuple

import jax
import jax.numpy as jnp
from jax import lax
from jax.experimental import pallas as pl
from jax.experimental.pallas import tpu as pltpu

DA_HEADS = 4
DA_HEAD_DIM = 64
DA_WIDTH = DA_HEADS * 2 * DA_HEAD_DIM
DN_HEADS = 4
DN_HEAD_DIM = 128
DN_WIDTH = DN_HEADS * DN_HEAD_DIM
CONV_K = 5
CHUNK = 64
ROPE_THETA = 10000.0
EPS = 1e-6
LOG2_E = math.log2(math.e)

LANES = 128
SUBLANES = 8
VMEM_LIMIT_BYTES = 56 * 1024 * 1024

D_MODEL = 1024
MAIN_DA_Q = 0
MAIN_DA_K = MAIN_DA_Q + DA_WIDTH // LANES
MAIN_DA_V = MAIN_DA_K + DA_WIDTH // LANES
MAIN_DN = MAIN_DA_V + DA_WIDTH // LANES
COL_GATE = 0
COL_DN_Q = 2 * D_MODEL // LANES
COL_DN_K = COL_DN_Q + DN_WIDTH // LANES
COL_DN_V = COL_DN_K + DN_WIDTH // LANES
COL_DN_Z = COL_DN_V + DN_WIDTH // LANES
COL_STORED = COL_DN_Z + DN_WIDTH // LANES


def _params(*sem):
    return pltpu.CompilerParams(dimension_semantics=sem, vmem_limit_bytes=VMEM_LIMIT_BYTES)


def _layer_spec(tail, layer, **kw):
    zeros = (0,) * len(tail)
    return pl.BlockSpec((None,) + tuple(tail), lambda *_: (layer,) + zeros, **kw)


def _mod_spec(d, layer, k):
    return pl.BlockSpec((None, None, None, 1, d), lambda bi, *_: (layer, bi, k, 0, 0))


MOD_SHIFT1, MOD_SCALE1, MOD_GATE1, MOD_SHIFT2, MOD_SCALE2, MOD_GATE2 = range(6)


def _silu(x):
    return x * jax.nn.sigmoid(x)


def _softplus(x):
    return jnp.maximum(x, 0.0) + jnp.log(1.0 + jnp.exp(-jnp.abs(x)))


def _dot(a, b):
    return jnp.dot(a, b, preferred_element_type=jnp.float32)


def _dot_nt(a, b):
    return lax.dot_general(a, b, (((1,), (1,)), ((), ())), preferred_element_type=jnp.float32)


def _split3(x):
    hi = x.astype(jnp.bfloat16).astype(jnp.float32)
    r1 = x - hi
    mid = r1.astype(jnp.bfloat16).astype(jnp.float32)
    lo = (r1 - mid).astype(jnp.bfloat16).astype(jnp.float32)
    return hi, mid, lo


def _adaln_kernel(c_ref, w_ref, b_ref, o_ref):
    c_act = _silu(c_ref[...])
    o_ref[0] = jnp.dot(c_act, w_ref[0], preferred_element_type=jnp.float32,
                       precision=lax.Precision.HIGHEST) + b_ref[0]


def _adaln(c_pad, ada_w, ada_b):
    depth, d, n = ada_w.shape
    tn = 1536
    return pl.pallas_call(
        _adaln_kernel,
        grid=(depth, n // tn),
        in_specs=[
            pl.BlockSpec((SUBLANES, d), lambda l, j: (0, 0)),
            pl.BlockSpec((1, d, tn), lambda l, j: (l, 0, j)),
            pl.BlockSpec((1, 1, tn), lambda l, j: (l, 0, j)),
        ],
        out_specs=pl.BlockSpec((1, SUBLANES, tn), lambda l, j: (l, 0, j)),
        out_shape=jax.ShapeDtypeStruct((depth, SUBLANES, n), jnp.float32),
        compiler_params=_params("parallel", "parallel"),
        name="adaln",
    )(c_pad, ada_w, ada_b.reshape(depth, 1, n))


def _modulated_norm(x, gain, scale, shift):
    y = x * lax.rsqrt(jnp.mean(x * x, axis=-1, keepdims=True) + EPS)
    return (y * gain) * (1.0 + scale) + shift


INPROJ_COLS = 512


def _norm_rope(x, gain, cos, sin):
    lane = lax.broadcasted_iota(jnp.int32, (1, LANES), 1)
    lane_col = lax.broadcasted_iota(jnp.int32, (LANES, 1), 0)
    same_map = ((lane_col < DA_HEAD_DIM) == (lane < DA_HEAD_DIM)).astype(jnp.bfloat16)
    first_half = (lane % DA_HEAD_DIM) < (DA_HEAD_DIM // 2)
    sq = x * x
    hi = sq.astype(jnp.bfloat16)
    lo = (sq - hi.astype(jnp.float32)).astype(jnp.bfloat16)
    ms = (_dot(hi, same_map) + _dot(lo, same_map)) * (1.0 / DA_HEAD_DIM)
    y = x * lax.rsqrt(ms + EPS) * gain
    fwd = pltpu.roll(y, LANES - DA_HEAD_DIM // 2, axis=1)
    bwd = pltpu.roll(y, DA_HEAD_DIM // 2, axis=1)
    return y * cos + jnp.where(first_half, fwd, bwd) * sin


def _inproj_kernel(x_ref, g_ref, sc_ref, sh_ref, w_ref, wg_ref, wba_ref, cos_ref, sin_ref, qg_ref, kg_ref,
                   o_ref, ba_ref, qo_ref, kt_ref, vo_ref):
    h = _modulated_norm(x_ref[0], g_ref[...], sc_ref[...], sh_ref[...]).astype(jnp.bfloat16)

    def cols(base):
        return _dot(h, w_ref[:, base * LANES:base * LANES + DA_WIDTH])

    q_all, k_all, v_all = cols(MAIN_DA_Q), cols(MAIN_DA_K), cols(MAIN_DA_V)
    cos, sin = cos_ref[...], sin_ref[...]
    left = lax.broadcasted_iota(jnp.int32, (1, LANES), 1) < DA_HEAD_DIM

    def q_piece(hh, sl):
        q = _norm_rope(q_all[:, sl], qg_ref[...], cos, sin) * (DA_HEAD_DIM ** -0.5 * LOG2_E)
        qo_ref[0, hh, 0] = jnp.where(left, q, 0.0).astype(qo_ref.dtype)
        qo_ref[0, hh, 1] = jnp.where(left, 0.0, q).astype(qo_ref.dtype)

    def k_piece(hh, sl):
        kt_ref[0, hh] = _norm_rope(k_all[:, sl], kg_ref[...], cos, sin).T.astype(kt_ref.dtype)

    def v_piece(hh, sl):
        v = v_all[:, sl].astype(vo_ref.dtype)
        vo_ref[0, hh] = jnp.concatenate([v, jnp.ones_like(v)], axis=1)

    pieces = [functools.partial(f, hh, slice(hh * LANES, (hh + 1) * LANES))
              for hh in range(DA_HEADS) for f in (q_piece, k_piece, v_piece)]
    srcs = ([(wg_ref, c) for c in range(0, wg_ref.shape[1], INPROJ_COLS)]
            + [(w_ref, c) for c in range(MAIN_DN * LANES, w_ref.shape[1], INPROJ_COLS)])
    for n, (src, c) in enumerate(srcs):
        out_cols = slice(n * INPROJ_COLS, (n + 1) * INPROJ_COLS)
        o_ref[0, :, out_cols] = _dot(h, src[:, c:c + INPROJ_COLS]).astype(o_ref.dtype)
        for piece in pieces[n * len(pieces) // len(srcs):(n + 1) * len(pieces) // len(srcs)]:
            piece()
    ba_ref[0] = _dot(h, wba_ref[...])


def _inproj(x, gains, mod, w, w_gate, w_ba, cos, sin_signed, qk_gains, layer, tm):
    b, s, d = x.shape
    n, n_gate = w.shape[2], w_gate.shape[2]
    nh = DA_HEADS
    n_store = COL_STORED * LANES
    assert n_gate % INPROJ_COLS == 0 and (n - MAIN_DN * LANES) % INPROJ_COLS == 0
    assert n_store == n_gate + n - MAIN_DN * LANES
    resident = dict(pipeline_mode=pl.Buffered(1))
    bf16 = jnp.bfloat16
    return pl.pallas_call(
        _inproj_kernel,
        grid=(b, s // tm),
        in_specs=[
            pl.BlockSpec((1, tm, d), lambda bi, i: (bi, i, 0)),
            _layer_spec((1, d), layer),
            _mod_spec(d, layer, MOD_SCALE1),
            _mod_spec(d, layer, MOD_SHIFT1),
            _layer_spec((d, n), layer, **resident),
            _layer_spec((d, n_gate), layer, **resident),
            _layer_spec((d, LANES), layer, **resident),
            pl.BlockSpec((tm, LANES), lambda bi, i: (i, 0)),
            pl.BlockSpec((tm, LANES), lambda bi, i: (i, 0)),
            pl.BlockSpec((None, None, 1, LANES), lambda bi, i: (layer, 0, 0, 0)),
            pl.BlockSpec((None, None, 1, LANES), lambda bi, i: (layer, 1, 0, 0)),
        ],
        out_specs=[pl.BlockSpec((1, tm, n_store), lambda bi, i: (bi, i, 0)),
                   pl.BlockSpec((1, tm, LANES), lambda bi, i: (bi, i, 0)),
                   pl.BlockSpec((1, nh, 2, tm, LANES), lambda bi, i: (bi, 0, 0, i, 0)),
                   pl.BlockSpec((1, nh, LANES, tm), lambda bi, i: (bi, 0, 0, i)),
                   pl.BlockSpec((1, nh, tm, 2 * LANES), lambda bi, i: (bi, 0, i, 0))],
        out_shape=[jax.ShapeDtypeStruct((b, s, n_store), bf16),
                   jax.ShapeDtypeStruct((b, s, LANES), jnp.float32),
                   jax.ShapeDtypeStruct((b, nh, 2, s, LANES), bf16),
                   jax.ShapeDtypeStruct((b, nh, LANES, s), bf16),
                   jax.ShapeDtypeStruct((b, nh, s, 2 * LANES), bf16)],
        compiler_params=_params("parallel", "parallel"),
        name="inproj",
    )(x, gains, mod, mod, w, w_gate, w_ba, cos, sin_signed, qk_gains, qk_gains)


ATTN_UNROLL = 8
ATTN_ROW_BLOCK = 32


def _attn_kernel(q_ref, kt_ref, v_ref, lam_ref, g_ref, o_ref, s_ref, p_ref, m_ref, alpha_ref, acc_ref,
                 *, tq, tk, lambda_init):
    s_len = v_ref.shape[2]
    q = q_ref[0, 0].reshape(2 * tq, LANES)
    n_kv = s_len // tk
    groups = tk // LANES
    rb = ATTN_ROW_BLOCK

    m_ref[...] = jnp.full(m_ref.shape, -jnp.inf, jnp.float32)
    acc_ref[...] = jnp.zeros(acc_ref.shape, jnp.float32)

    def scores(j, slot):
        s_ref[slot] = _dot(q, kt_ref[0, 0, :, pl.ds(pl.multiple_of(j * tk, tk), tk)])

    def update(j, slot):
        for r in range(2 * tq // rb):
            rows = slice(r * rb, (r + 1) * rb)
            sg = [s_ref[slot, rows, g * LANES:(g + 1) * LANES] for g in range(groups)]
            m_old = m_ref[rows, :]
            m_new = jnp.maximum(m_old, jnp.max(functools.reduce(jnp.maximum, sg), axis=-1, keepdims=True))
            alpha_ref[rows, :] = jnp.exp2(m_old - m_new)
            m_ref[rows, :] = m_new
            for g in range(groups):
                p_ref[rows, g * LANES:(g + 1) * LANES] = jnp.exp2(sg[g] - m_new).astype(jnp.bfloat16)
        alpha = alpha_ref[...]
        pv = _dot(p_ref[...], v_ref[0, 0, pl.ds(pl.multiple_of(j * tk, tk), tk), :])
        acc_ref[...] = jnp.concatenate([alpha, alpha], axis=1) * acc_ref[...] + pv

    def body(jj, carry):
        j = ATTN_UNROLL * jj
        for k in range(ATTN_UNROLL):
            scores(jnp.minimum(j + k + 1, n_kv - 1), (k + 1) % 2)
            update(j + k, k % 2)
        return carry

    scores(0, 0)
    lax.fori_loop(0, n_kv // ATTN_UNROLL, body, 0)
    o = acc_ref[:, :LANES] / acc_ref[:, LANES:]
    lq = lam_ref[...]
    lam = (jnp.exp(jnp.sum(lq[0:1] * lq[1:2], axis=-1, keepdims=True))
           - jnp.exp(jnp.sum(lq[2:3] * lq[3:4], axis=-1, keepdims=True)) + lambda_init)
    d = o[:tq] - lam * o[tq:]
    y = d * lax.rsqrt(jnp.mean(d * d, axis=-1, keepdims=True) + EPS) * g_ref[...]
    o_ref[0] = (y * (1.0 - lambda_init)).astype(o_ref.dtype)


def _attn(qs, kt, v, lam_params, subln_gain, layer, lambda_init, tq, tk):
    b, h, _, s, _ = qs.shape
    assert s % (tk * ATTN_UNROLL) == 0 and ATTN_UNROLL % 2 == 0 and (2 * tq) % ATTN_ROW_BLOCK == 0
    return pl.pallas_call(
        functools.partial(_attn_kernel, tq=tq, tk=tk, lambda_init=lambda_init),
        grid=(b, h, s // tq),
        in_specs=[
            pl.BlockSpec((1, 1, 2, tq, LANES), lambda bi, hi, i: (bi, hi, 0, i, 0)),
            pl.BlockSpec((1, 1, LANES, s), lambda bi, hi, i: (bi, hi, 0, 0)),
            pl.BlockSpec((1, 1, s, 2 * LANES), lambda bi, hi, i: (bi, hi, 0, 0)),
            _layer_spec(lam_params.shape[1:], layer),
            _layer_spec((1, LANES), layer),
        ],
        out_specs=pl.BlockSpec((1, tq, LANES), lambda bi, hi, i: (bi, i, hi)),
        out_shape=jax.ShapeDtypeStruct((b, s, h * LANES), jnp.bfloat16),
        scratch_shapes=[pltpu.VMEM((2, 2 * tq, tk), jnp.float32), pltpu.VMEM((2 * tq, tk), jnp.bfloat16),
                        pltpu.VMEM((2 * tq, LANES), jnp.float32), pltpu.VMEM((2 * tq, LANES), jnp.float32),
                        pltpu.VMEM((2 * tq, 2 * LANES), jnp.float32)],
        compiler_params=_params("parallel", "parallel", "parallel"),
        name="attn",
    )(qs, kt, v, lam_params, subln_gain)


INV_BLOCK = 16
HALO = 16
GDN_UNROLL = 8


def _gdn_prep_kernel(q_ref, qp_ref, qn_ref, k_ref, kp_ref, kn_ref, v_ref, vp_ref, vn_ref,
                     cw_ref, ba_ref, alog_ref, dtb_ref,
                     wq_ref, u_ref, qkkd_ref, gl_ref,
                     qs_ref, ks_ref, vs_ref, bf_ref, bb_ref, gf_ref, gb_ref, xq_ref, xk_ref, xv_ref,
                     *, t, nblk):
    i = pl.program_id(1)
    pad = CONV_K // 2
    bf = jnp.bfloat16

    def conv_silu(h, x_ref, p_ref, n_ref, w, xp_ref):
        f32 = jnp.float32
        cols = slice(h * LANES, (h + 1) * LANES)
        xp_ref[h, 0:HALO, :] = jnp.where(i == 0, 0.0, p_ref[0, :, cols].astype(f32))
        xp_ref[h, HALO:HALO + t, :] = x_ref[0, :, cols].astype(f32)
        xp_ref[h, HALO + t:2 * HALO + t, :] = jnp.where(i == nblk - 1, 0.0, n_ref[0, :, cols].astype(f32))
        acc = xp_ref[h, HALO - pad:HALO - pad + t, :] * w[0:1, cols]
        for j in range(1, CONV_K):
            acc = acc + xp_ref[h, HALO - pad + j:HALO - pad + j + t, :] * w[j:j + 1, cols]
        return _silu(acc)

    def l2n(x):
        return x * lax.rsqrt(jnp.sum(x * x, axis=-1, keepdims=True) + EPS)

    cw = cw_ref[...]
    lane = lax.broadcasted_iota(jnp.int32, (1, LANES), 1)

    r3 = lax.broadcasted_iota(jnp.int32, (CHUNK, 3 * CHUNK), 0)
    c3 = lax.broadcasted_iota(jnp.int32, (CHUNK, 3 * CHUNK), 1) % CHUNK
    tril3 = (r3 >= c3).astype(bf)
    triu3 = (r3 <= c3).astype(bf)

    ba = ba_ref[0]
    g_all = -jnp.exp(alog_ref[...]) * _softplus(ba + dtb_ref[...])
    beta_all = jax.nn.sigmoid(ba)
    hi, mid, lo = _split3(g_all)
    gcf, gcb = [], []
    for c in range(t // CHUNK):
        rows = slice(c * CHUNK, (c + 1) * CHUNK)
        parts = jnp.concatenate([hi[rows], mid[rows], lo[rows]], axis=0).astype(bf)
        gcf.append(_dot(tril3, parts))
        gcb.append(_dot(triu3, parts))

    def pick(x, idx):
        col = jnp.sum(jnp.where(lane == idx, x, 0.0), axis=-1, keepdims=True)
        return jnp.broadcast_to(col, (t, LANES))

    gcf = jnp.concatenate(gcf, axis=0)
    gcb = jnp.concatenate(gcb, axis=0)

    row = lax.broadcasted_iota(jnp.int32, (CHUNK, 2 * CHUNK), 0)
    col2 = lax.broadcasted_iota(jnp.int32, (CHUNK, 2 * CHUNK), 1)
    left = col2 < CHUNK
    colm = col2 % CHUNK
    keep2 = (left & (row >= colm)) | (~left & (row <= colm))
    strict2 = keep2 & (row != colm)
    diag2 = row == colm
    eye2 = diag2.astype(jnp.float32)
    same2 = (row // INV_BLOCK) == (colm // INV_BLOCK)
    row4 = lax.broadcasted_iota(jnp.int32, (2 * CHUNK, 2 * CHUNK), 0)
    col4 = lax.broadcasted_iota(jnp.int32, (2 * CHUNK, 2 * CHUNK), 1)
    bd_mask = ((row4 < CHUNK) == (col4 < CHUNK)).astype(bf)

    def blockdiag(y):
        yb = y.astype(bf)
        return jnp.concatenate([yb, yb], axis=0) * bd_mask

    def st_load(h, c):
        rows = pl.ds(pl.multiple_of(c * CHUNK, CHUNK), CHUNK)
        v = dict(h=h, c=c, rows=rows, qc=qs_ref[h, rows, :], kc=ks_ref[h, rows, :], vc=vs_ref[h, rows, :],
                 beta_f=bf_ref[h, rows, :], beta_b=bb_ref[h, rows, :],
                 gc_f=gf_ref[h, rows, :], gc_b=gb_ref[h, rows, :])
        kcb = v['kc'].astype(bf)
        kk = jnp.concatenate([kcb, kcb], axis=0)
        v['gram2'] = _dot_nt(kcb, kk)
        v['qk2'] = _dot_nt(v['qc'].astype(bf), kk)
        return v

    def st_lmat(v):
        gc2 = jnp.where(left, v['gc_f'], v['gc_b'])
        gc_row = jnp.sum(jnp.where(diag2, gc2, 0.0), axis=0, keepdims=True)
        decay2 = jnp.where(keep2, jnp.exp(jnp.where(keep2, gc2 - gc_row, 0.0)), 0.0)
        l2 = jnp.where(strict2, jnp.where(left, v['beta_f'], v['beta_b']) * v['gram2'] * decay2, 0.0)
        ld = jnp.where(same2, l2, 0.0)
        v.update(qkm=(v['qk2'] * decay2).astype(bf), lo=l2 - ld, p=eye2 - ld,
                 m=_dot(ld.astype(bf), blockdiag(ld)))
        del v['gram2'], v['qk2']
        return v

    def st_level(v):
        r = _dot(jnp.concatenate([v['p'], v['m']], axis=0).astype(bf), blockdiag(v['m']))
        v.update(p=v['p'] + r[:CHUNK], m=r[CHUNK:])
        return v

    def st_dinv(v):
        v['d_inv'] = v['p'] + _dot(v['p'].astype(bf), blockdiag(v['m']))
        return v

    def st_n(v):
        v['n'] = _dot(v['d_inv'].astype(bf), blockdiag(v['lo']))
        return v

    def st_nsq(v):
        v['nsq'] = _dot(v['n'].astype(bf), blockdiag(v['n']))
        return v

    def st_pn(v):
        pn = eye2 - v['n']
        v['pn'] = pn + _dot(pn.astype(bf), blockdiag(v['nsq']))
        return v

    def st_t(v):
        v['t2'] = _dot(v['pn'].astype(bf), blockdiag(v['d_inv']))
        return v

    def st_uw(v):
        kc, vc = v['kc'], v['vc']
        v['egc_f'] = jnp.exp(v['gc_f'])
        v['egc_b'] = jnp.exp(v['gc_b'])
        rhs = jnp.concatenate([
            jnp.concatenate([vc * v['beta_f'], kc * (v['beta_f'] * v['egc_f'])], axis=1),
            jnp.concatenate([vc * v['beta_b'], kc * (v['beta_b'] * v['egc_b'])], axis=1)], axis=0).astype(bf)
        v['uw'] = _dot(blockdiag(v['t2']), rhs)
        return v

    def st_store(v):
        h, c, rows, uw, qc, kc = v['h'], v['c'], v['rows'], v['uw'], v['qc'], v['kc']
        r2 = pl.multiple_of(c * 2 * CHUNK, 2 * CHUNK)
        for d, egc in ((0, v['egc_f']), (1, v['egc_b'])):
            u_ref[d, 0, h, rows, :] = uw[d * CHUNK:(d + 1) * CHUNK, :LANES].astype(bf)
            wq_ref[d, 0, h, pl.ds(r2, CHUNK), :] = uw[d * CHUNK:(d + 1) * CHUNK, LANES:].astype(bf)
            wq_ref[d, 0, h, pl.ds(r2 + CHUNK, CHUNK), :] = (qc * egc).astype(bf)
        gtot_f = v['gc_f'][CHUNK - 1:CHUNK, :]
        gtot_b = v['gc_b'][0:1, :]
        kd = jnp.concatenate([kc * jnp.exp(gtot_f - v['gc_f']), kc * jnp.exp(gtot_b - v['gc_b'])], axis=0)
        r3_ = pl.multiple_of(c * 3 * CHUNK, CHUNK)
        qkkd_ref[0, h, pl.ds(r3_, CHUNK), :] = v['qkm']
        qkkd_ref[0, h, pl.ds(r3_ + CHUNK, 2 * CHUNK), :] = kd.T.astype(bf)
        r8 = pl.multiple_of(c * SUBLANES, SUBLANES)
        gl_ref[0, 0, h, pl.ds(r8, SUBLANES), :] = jnp.broadcast_to(jnp.exp(gtot_f), (SUBLANES, LANES))
        gl_ref[1, 0, h, pl.ds(r8, SUBLANES), :] = jnp.broadcast_to(jnp.exp(gtot_b), (SUBLANES, LANES))

    levels = [st_level] * (int(math.log2(INV_BLOCK)) - 2)
    stages = [st_lmat] + levels + [st_dinv, st_n, st_nsq, st_pn, st_t, st_uw]

    for h in range(DN_HEADS):
        qs_ref[h] = l2n(conv_silu(h, q_ref, qp_ref, qn_ref, cw[0], xq_ref)) * (DN_HEAD_DIM ** -0.5)
        ks_ref[h] = l2n(conv_silu(h, k_ref, kp_ref, kn_ref, cw[1], xk_ref))
        vs_ref[h] = conv_silu(h, v_ref, vp_ref, vn_ref, cw[2], xv_ref)
        bf_ref[h] = pick(beta_all, h)
        bb_ref[h] = pick(beta_all, DN_HEADS + h)
        gf_ref[h] = pick(gcf, 2 * DN_HEADS + h)
        gb_ref[h] = pick(gcb, 3 * DN_HEADS + h)

        def trip(j, carry, h=h):
            vals = [st_load(h, j * GDN_UNROLL + k) for k in range(GDN_UNROLL)]
            for stage in stages:
                vals = [stage(v) for v in vals]
            for v in vals:
                st_store(v)
            return carry

        lax.fori_loop(0, t // (CHUNK * GDN_UNROLL), trip, 0)


def _gdn_prep(proj, ba, conv_w, a_log_row, dt_bias_row, layer, t):
    b, s, _ = proj.shape
    nh = DN_HEADS
    nblk = s // t
    n = s // CHUNK
    tpb = t // HALO
    wcol = DN_WIDTH // LANES

    def main(base):
        return pl.BlockSpec((1, t, DN_WIDTH), lambda bi, i: (bi, i, base // wcol))

    def prev(base):
        return pl.BlockSpec((1, HALO, DN_WIDTH),
                            lambda bi, i: (bi, jnp.maximum(i * tpb - 1, 0), base // wcol))

    def nxt(base):
        return pl.BlockSpec((1, HALO, DN_WIDTH),
                            lambda bi, i: (bi, jnp.minimum((i + 1) * tpb, s // HALO - 1), base // wcol))

    in_specs = []
    for base in (COL_DN_Q, COL_DN_K, COL_DN_V):
        in_specs += [main(base), prev(base), nxt(base)]
    in_specs += [
        _layer_spec((3, CONV_K, DN_WIDTH), layer),
        pl.BlockSpec((1, t, LANES), lambda bi, i: (bi, i, 0)),
        _layer_spec((1, LANES), layer),
        _layer_spec((1, LANES), layer),
    ]
    out_specs = [
        pl.BlockSpec((2, 1, nh, 2 * t, LANES), lambda bi, i: (0, bi, 0, i, 0)),
        pl.BlockSpec((2, 1, nh, t, LANES), lambda bi, i: (0, bi, 0, i, 0)),
        pl.BlockSpec((1, nh, 3 * t, LANES), lambda bi, i: (bi, 0, i, 0)),
        pl.BlockSpec((2, 1, nh, t // CHUNK * SUBLANES, LANES), lambda bi, i: (0, bi, 0, i, 0)),
    ]
    out_shape = [
        jax.ShapeDtypeStruct((2, b, nh, 2 * s, LANES), jnp.bfloat16),
        jax.ShapeDtypeStruct((2, b, nh, s, LANES), jnp.bfloat16),
        jax.ShapeDtypeStruct((b, nh, 3 * s, LANES), jnp.bfloat16),
        jax.ShapeDtypeStruct((2, b, nh, n * SUBLANES, LANES), jnp.float32),
    ]
    proj9 = [proj] * 9
    return pl.pallas_call(
        functools.partial(_gdn_prep_kernel, t=t, nblk=nblk),
        grid=(b, nblk),
        in_specs=in_specs,
        out_specs=out_specs,
        out_shape=out_shape,
        scratch_shapes=([pltpu.VMEM((nh, t, LANES), jnp.float32)] * 7
                        + [pltpu.VMEM((nh, t + 2 * HALO, LANES), jnp.float32)] * 3),
        compiler_params=_params("parallel", "parallel"),
        name="gdn_prep",
    )(*proj9, conv_w, ba, a_log_row, dt_bias_row)


def _gdn_scan_kernel(wq_f, u_f, qkkd_f, gl_f, wq_b, u_b, qkkd_b, gl_b, o_f, o_b, state_ref, *, t):
    nc = t // CHUNK
    nb = u_f.shape[1]

    @pl.when(pl.program_id(0) == 0)
    def _():
        state_ref[...] = jnp.zeros_like(state_ref)

    streams = ((0, wq_f, u_f, qkkd_f, gl_f, o_f), (1, wq_b, u_b, qkkd_b, gl_b, o_b))
    zeros = jnp.zeros((CHUNK, DN_HEAD_DIM), jnp.bfloat16)

    def body(c, carry):
        work = []
        for d, wq, u, qkkd, gl, o in streams:
            cc = c if d == 0 else nc - 1 - c
            r2 = pl.multiple_of(cc * 2 * CHUNK, 2 * CHUNK)
            for bi in range(nb):
                for hh in range(DN_HEADS):
                    st = state_ref[d, bi, hh]
                    ws = _dot(wq[0, bi, hh, pl.ds(r2, 2 * CHUNK), :], st.astype(jnp.bfloat16))
                    work.append((d, bi, hh, cc, u, qkkd, gl, o, st, ws))
        for d, bi, hh, cc, u, qkkd, gl, o, st, ws in work:
            r0 = pl.multiple_of(cc * CHUNK, CHUNK)
            r3 = pl.multiple_of(cc * 3 * CHUNK, CHUNK)
            r8 = pl.multiple_of(cc * SUBLANES, SUBLANES)
            v_new = (u[0, bi, hh, pl.ds(r0, CHUNK), :].astype(jnp.float32) - ws[:CHUNK]).astype(jnp.bfloat16)
            rhs = jnp.concatenate([v_new, zeros] if d == 0 else [zeros, v_new], axis=0)
            r = _dot(qkkd[bi, hh, pl.ds(r3, 3 * CHUNK), :], rhs)
            o[bi, hh, pl.ds(r0, CHUNK), :] = (ws[CHUNK:] + r[:CHUNK]).astype(o.dtype)
            decay = gl[0, bi, hh, pl.ds(r8, SUBLANES), :][0:1]
            state_ref[d, bi, hh] = st * decay + r[CHUNK:]
        return carry

    lax.fori_loop(0, nc, body, 0)


def _gdn_scan(wq, u, qkkd, gl, t):
    _, b, nh, s, _ = u.shape
    nblk = s // t
    g8 = t // CHUNK * SUBLANES

    def specs(d):
        blk = (lambda i: i) if d == 0 else (lambda i: nblk - 1 - i)
        idx = lambda i: (d, 0, 0, blk(i), 0)
        idx4 = lambda i: (0, 0, blk(i), 0)
        return [
            pl.BlockSpec((1, b, nh, 2 * t, LANES), idx),
            pl.BlockSpec((1, b, nh, t, LANES), idx),
            pl.BlockSpec((b, nh, 3 * t, LANES), idx4),
            pl.BlockSpec((1, b, nh, g8, LANES), idx),
        ], pl.BlockSpec((b, nh, t, LANES), idx4)

    in_f, out_f = specs(0)
    in_b, out_b = specs(1)
    o_shape = jax.ShapeDtypeStruct((b, nh, s, LANES), jnp.bfloat16)
    return pl.pallas_call(
        functools.partial(_gdn_scan_kernel, t=t),
        grid=(nblk,),
        in_specs=in_f + in_b,
        out_specs=[out_f, out_b],
        out_shape=[o_shape, o_shape],
        scratch_shapes=[pltpu.VMEM((2, b, nh, DN_HEAD_DIM, DN_HEAD_DIM), jnp.float32)],
        compiler_params=_params("arbitrary"),
        name="gdn_scan",
    )(wq, u, qkkd, gl, wq, u, qkkd, gl)


def _merge_kernel(of_ref, ob_ref, z_ref, ga_ref, gb_ref, ya_ref, x_ref, gt_ref, og_ref,
                  wa_ref, wb_ref, wo_ref, o_ref):
    og = og_ref[...]
    z = z_ref[0].astype(jnp.float32)
    parts = []
    for hh in range(DN_HEADS):
        o = of_ref[0, hh].astype(jnp.float32) + ob_ref[0, hh].astype(jnp.float32)
        y = o * lax.rsqrt(jnp.mean(o * o, axis=-1, keepdims=True) + EPS) * og
        parts.append((y * _silu(z[:, hh * LANES:(hh + 1) * LANES])).astype(jnp.bfloat16))
    yb = jnp.concatenate(parts, axis=1)
    pa = _dot(ya_ref[0], wa_ref[...])
    pb = _dot(yb, wb_ref[...])
    merged = (jax.nn.sigmoid(ga_ref[0].astype(jnp.float32)) * pa
              + jax.nn.sigmoid(gb_ref[0].astype(jnp.float32)) * pb)
    o_ref[0] = x_ref[0] + gt_ref[...] * _dot(merged.astype(jnp.bfloat16), wo_ref[...])


def _merge(o_f, o_b, proj, y_a, x, mod, out_gain, w_a, w_b, w_o, layer, tm):
    b, s, d = x.shape
    nh = DN_HEADS
    zw = DN_WIDTH // LANES
    return pl.pallas_call(
        _merge_kernel,
        grid=(b, s // tm),
        in_specs=[
            pl.BlockSpec((1, nh, tm, LANES), lambda bi, i: (bi, 0, i, 0)),
            pl.BlockSpec((1, nh, tm, LANES), lambda bi, i: (bi, 0, i, 0)),
            pl.BlockSpec((1, tm, DN_WIDTH), lambda bi, i: (bi, i, COL_DN_Z // zw)),
            pl.BlockSpec((1, tm, d), lambda bi, i: (bi, i, 0)),
            pl.BlockSpec((1, tm, d), lambda bi, i: (bi, i, 1)),
            pl.BlockSpec((1, tm, DA_WIDTH), lambda bi, i: (bi, i, 0)),
            pl.BlockSpec((1, tm, d), lambda bi, i: (bi, i, 0)),
            _mod_spec(d, layer, MOD_GATE1),
            _layer_spec((1, LANES), layer),
            _layer_spec(w_a.shape[1:], layer),
            _layer_spec(w_b.shape[1:], layer),
            _layer_spec(w_o.shape[1:], layer),
        ],
        out_specs=pl.BlockSpec((1, tm, d), lambda bi, i: (bi, i, 0)),
        out_shape=jax.ShapeDtypeStruct((b, s, d), jnp.float32),
        compiler_params=_params("parallel", "parallel"),
        name="merge",
    )(o_f, o_b, proj, proj, proj, y_a, x, mod, out_gain, w_a, w_b, w_o)


def _mlp_kernel(x_ref, g_ref, sc_ref, sh_ref, gt_ref, w1_ref, w2_ref, o_ref, h_ref, acc_ref):
    j = pl.program_id(2)

    @pl.when(j == 0)
    def _():
        h_ref[...] = _modulated_norm(x_ref[0], g_ref[...], sc_ref[...], sh_ref[...]).astype(h_ref.dtype)
        acc_ref[...] = jnp.zeros_like(acc_ref)

    a = jnp.maximum(_dot(h_ref[...], w1_ref[...]), 0.0)
    acc_ref[...] += _dot((a * a).astype(jnp.bfloat16), w2_ref[...])

    @pl.when(j == pl.num_programs(2) - 1)
    def _():
        o_ref[0] = x_ref[0] + gt_ref[...] * acc_ref[...]


def _mlp(x, gains, mod, w1, w2, layer, tm, tf):
    b, s, d = x.shape
    f = w1.shape[2]
    return pl.pallas_call(
        _mlp_kernel,
        grid=(b, s // tm, f // tf),
        in_specs=[
            pl.BlockSpec((1, tm, d), lambda bi, i, j: (bi, i, 0)),
            _layer_spec((1, d), layer),
            _mod_spec(d, layer, MOD_SCALE2),
            _mod_spec(d, layer, MOD_SHIFT2),
            _mod_spec(d, layer, MOD_GATE2),
            pl.BlockSpec((None, d, tf), lambda bi, i, j: (layer, 0, j)),
            pl.BlockSpec((None, tf, d), lambda bi, i, j: (layer, j, 0)),
        ],
        out_specs=pl.BlockSpec((1, tm, d), lambda bi, i, j: (bi, i, 0)),
        out_shape=jax.ShapeDtypeStruct((b, s, d), jnp.float32),
        scratch_shapes=[pltpu.VMEM((tm, d), jnp.bfloat16), pltpu.VMEM((tm, d), jnp.float32)],
        compiler_params=_params("parallel", "parallel", "arbitrary"),
        name="mlp",
    )(x, gains, mod, mod, mod, w1, w2)


def _split_w_in(w_in, d_model):
    assert d_model == D_MODEL
    main = 3 * DA_WIDTH + 4 * DN_WIDTH
    small = 4 * DN_HEADS
    bf16 = jnp.bfloat16
    pad = jnp.zeros(w_in.shape[:-1] + (LANES - small,), bf16)
    return (w_in[..., :main].astype(bf16), w_in[..., main + small:].astype(bf16),
            jnp.concatenate([w_in[..., main:main + small].astype(bf16), pad], axis=-1))


def _rope_tables(seq):
    pos = jnp.arange(seq, dtype=jnp.float32)
    inv = 1.0 / (ROPE_THETA ** (jnp.arange(0, DA_HEAD_DIM, 2, dtype=jnp.float32) / DA_HEAD_DIM))
    ang = pos[:, None] * inv[None, :]
    reps = LANES // ang.shape[1]
    sign = jnp.where((jnp.arange(LANES) % DA_HEAD_DIM) < DA_HEAD_DIM // 2, -1.0, 1.0)
    return jnp.tile(jnp.cos(ang), (1, reps)), jnp.tile(jnp.sin(ang), (1, reps)) * sign


class _Tiles(NamedTuple):
    proj_rows: int
    mlp_rows: int
    mlp_cols: int
    gdn_rows: int
    attn_q: int
    attn_k: int


def _tile_sizes(s):
    return _Tiles(proj_rows=min(512, s), mlp_rows=min(1024, s), mlp_cols=1024,
                  gdn_rows=min(CHUNK * GDN_UNROLL, s), attn_q=min(512, s), attn_k=min(512, s // ATTN_UNROLL))


def kernel(x, c, ada_w, ada_b, norm1_g, norm2_g, w_in, dn_conv, dn_A_log, dn_dt_bias, dn_out_gain,
           qk_norm_gain, diff_lambda, diff_subln_gain, w_branch_a, w_branch_b, w_out, w_mlp1, w_mlp2):
    b, s, d = x.shape
    depth = ada_w.shape[0]
    assert s % 512 == 0 and d % LANES == 0 and b <= SUBLANES
    tiles = _tile_sizes(s)

    c_pad = jnp.zeros((SUBLANES, d), jnp.float32).at[:b].set(c)
    mod = _adaln(c_pad, ada_w, ada_b)[:, :b].reshape(depth, b, 6, 1, d)
    cos, sin_signed = _rope_tables(s)

    bf16 = jnp.bfloat16
    small = 4 * DN_HEADS
    w_main, w_gate, w_ba = _split_w_in(w_in, d)
    norm1, norm2 = norm1_g.reshape(depth, 1, d), norm2_g.reshape(depth, 1, d)
    qk_gains = jnp.tile(qk_norm_gain, (1, 1, 2)).reshape(depth, 2, 1, LANES)
    subln_gain = diff_subln_gain.reshape(depth, 1, LANES)
    out_gain = dn_out_gain.reshape(depth, 1, LANES)
    conv_w = dn_conv.reshape(depth, CONV_K, 3, DN_WIDTH).transpose(0, 2, 1, 3)
    rows = jnp.zeros((depth, 1, LANES), jnp.float32)
    a_log_rows = rows.at[:, 0, small // 2:small].set(dn_A_log.reshape(depth, -1))
    dt_bias_rows = rows.at[:, 0, small // 2:small].set(dn_dt_bias.reshape(depth, -1))
    w_a, w_b, w_o = w_branch_a.astype(bf16), w_branch_b.astype(bf16), w_out.astype(bf16)
    w1, w2 = w_mlp1.astype(bf16), w_mlp2.astype(bf16)

    for l in range(depth):
        lambda_init = 0.8 - 0.6 * math.exp(-0.3 * l)
        proj, ba, qs, kt, v = _inproj(x, norm1, mod, w_main, w_gate, w_ba, cos, sin_signed, qk_gains, l,
                                      tiles.proj_rows)
        y_a = _attn(qs, kt, v, diff_lambda, subln_gain, l, lambda_init, tiles.attn_q, tiles.attn_k)
        wq, u, qkkd, gl = _gdn_prep(proj, ba, conv_w, a_log_rows, dt_bias_rows, l, tiles.gdn_rows)
        o_f, o_b = _gdn_scan(wq, u, qkkd, gl, tiles.gdn_rows)
        x = _merge(o_f, o_b, proj, y_a, x, mod, out_gain, w_a, w_b, w_o, l, tiles.proj_rows)
        x = _mlp(x, norm2, mod, w1, w2, l, tiles.mlp_rows, tiles.mlp_cols)
    return x
```

```python
import functools
import math
from typing import NamedTuple

import jax
import jax.numpy as jnp
from jax import lax
from jax.experimental import pallas as pl
from jax.experimental.pallas import tpu as pltpu

DA_HEADS = 4
DA_HEAD_DIM = 64
DA_WIDTH = DA_HEADS * 2 * DA_HEAD_DIM
DN_HEADS = 4
DN_HEAD_DIM = 128
DN_WIDTH = DN_HEADS * DN_HEAD_DIM
CONV_K = 5
CHUNK = 64
ROPE_THETA = 10000.0
EPS = 1e-6
LOG2_E = math.log2(math.e)

LANES = 128
SUBLANES = 8
VMEM_LIMIT_BYTES = 56 * 1024 * 1024

D_MODEL = 1024
MAIN_DA_Q = 0
MAIN_DA_K = MAIN_DA_Q + DA_WIDTH // LANES
MAIN_DA_V = MAIN_DA_K + DA_WIDTH // LANES
MAIN_DN = MAIN_DA_V + DA_WIDTH // LANES
COL_GATE = 0
COL_DN_Q = 2 * D_MODEL // LANES
COL_DN_K = COL_DN_Q + DN_WIDTH // LANES
COL_DN_V = COL_DN_K + DN_WIDTH // LANES
COL_DN_Z = COL_DN_V + DN_WIDTH // LANES
COL_STORED = COL_DN_Z + DN_WIDTH // LANES


def _params(*sem):
    return pltpu.CompilerParams(dimension_semantics=sem, vmem_limit_bytes=VMEM_LIMIT_BYTES)


def _layer_spec(tail, layer, **kw):
    zeros = (0,) * len(tail)
    return pl.BlockSpec((None,) + tuple(tail), lambda *_: (layer,) + zeros, **kw)


def _mod_spec(d, layer, k):
    return pl.BlockSpec((None, None, None, 1, d), lambda bi, *_: (layer, bi, k, 0, 0))


MOD_SHIFT1, MOD_SCALE1, MOD_GATE1, MOD_SHIFT2, MOD_SCALE2, MOD_GATE2 = range(6)


def _silu(x):
    return x * jax.nn.sigmoid(x)


def _softplus(x):
    return jnp.maximum(x, 0.0) + jnp.log(1.0 + jnp.exp(-jnp.abs(x)))


def _dot(a, b):
    return jnp.dot(a, b, preferred_element_type=jnp.float32)


def _dot_nt(a, b):
    return lax.dot_general(a, b, (((1,), (1,)), ((), ())), preferred_element_type=jnp.float32)


def _split3(x):
    hi = x.astype(jnp.bfloat16).astype(jnp.float32)
    r1 = x - hi
    mid = r1.astype(jnp.bfloat16).astype(jnp.float32)
    lo = (r1 - mid).astype(jnp.bfloat16).astype(jnp.float32)
    return hi, mid, lo


def _adaln_kernel(c_ref, w_ref, b_ref, o_ref):
    c_act = _silu(c_ref[...])
    o_ref[0] = jnp.dot(c_act, w_ref[0], preferred_element_type=jnp.float32,
                       precision=lax.Precision.HIGHEST) + b_ref[0]


def _adaln(c_pad, ada_w, ada_b):
    depth, d, n = ada_w.shape
    tn = 1536
    return pl.pallas_call(
        _adaln_kernel,
        grid=(depth, n // tn),
        in_specs=[
            pl.BlockSpec((SUBLANES, d), lambda l, j: (0, 0)),
            pl.BlockSpec((1, d, tn), lambda l, j: (l, 0, j)),
            pl.BlockSpec((1, 1, tn), lambda l, j: (l, 0, j)),
        ],
        out_specs=pl.BlockSpec((1, SUBLANES, tn), lambda l, j: (l, 0, j)),
        out_shape=jax.ShapeDtypeStruct((depth, SUBLANES, n), jnp.float32),
        compiler_params=_params("parallel", "parallel"),
        name="adaln",
    )(c_pad, ada_w, ada_b.reshape(depth, 1, n))


def _modulated_norm(x, gain, scale, shift):
    y = x * lax.rsqrt(jnp.mean(x * x, axis=-1, keepdims=True) + EPS)
    return (y * gain) * (1.0 + scale) + shift


INPROJ_COLS = 512


def _norm_rope(x, gain, cos, sin):
    lane = lax.broadcasted_iota(jnp.int32, (1, LANES), 1)
    lane_col = lax.broadcasted_iota(jnp.int32, (LANES, 1), 0)
    same_map = ((lane_col < DA_HEAD_DIM) == (lane < DA_HEAD_DIM)).astype(jnp.bfloat16)
    first_half = (lane % DA_HEAD_DIM) < (DA_HEAD_DIM // 2)
    sq = x * x
    hi = sq.astype(jnp.bfloat16)
    lo = (sq - hi.astype(jnp.float32)).astype(jnp.bfloat16)
    ms = (_dot(hi, same_map) + _dot(lo, same_map)) * (1.0 / DA_HEAD_DIM)
    y = x * lax.rsqrt(ms + EPS) * gain
    fwd = pltpu.roll(y, LANES - DA_HEAD_DIM // 2, axis=1)
    bwd = pltpu.roll(y, DA_HEAD_DIM // 2, axis=1)
    return y * cos + jnp.where(first_half, fwd, bwd) * sin


def _inproj_kernel(x_ref, g_ref, sc_ref, sh_ref, w_ref, wg_ref, wba_ref, cos_ref, sin_ref, qg_ref, kg_ref,
                   o_ref, ba_ref, qo_ref, kt_ref, vo_ref):
    h = _modulated_norm(x_ref[0], g_ref[...], sc_ref[...], sh_ref[...]).astype(jnp.bfloat16)

    def cols(base):
        return _dot(h, w_ref[:, base * LANES:base * LANES + DA_WIDTH])

    q_all, k_all, v_all = cols(MAIN_DA_Q), cols(MAIN_DA_K), cols(MAIN_DA_V)
    cos, sin = cos_ref[...], sin_ref[...]
    left = lax.broadcasted_iota(jnp.int32, (1, LANES), 1) < DA_HEAD_DIM

    def q_piece(hh, sl):
        q = _norm_rope(q_all[:, sl], qg_ref[...], cos, sin) * (DA_HEAD_DIM ** -0.5 * LOG2_E)
        qo_ref[0, hh, 0] = jnp.where(left, q, 0.0).astype(qo_ref.dtype)
        qo_ref[0, hh, 1] = jnp.where(left, 0.0, q).astype(qo_ref.dtype)

    def k_piece(hh, sl):
        kt_ref[0, hh] = _norm_rope(k_all[:, sl], kg_ref[...], cos, sin).T.astype(kt_ref.dtype)

    def v_piece(hh, sl):
        v = v_all[:, sl].astype(vo_ref.dtype)
        vo_ref[0, hh] = jnp.concatenate([v, jnp.ones_like(v)], axis=1)

    pieces = [functools.partial(f, hh, slice(hh * LANES, (hh + 1) * LANES))
              for hh in range(DA_HEADS) for f in (q_piece, k_piece, v_piece)]
    srcs = ([(wg_ref, c) for c in range(0, wg_ref.shape[1], INPROJ_COLS)]
            + [(w_ref, c) for c in range(MAIN_DN * LANES, w_ref.shape[1], INPROJ_COLS)])
    for n, (src, c) in enumerate(srcs):
        out_cols = slice(n * INPROJ_COLS, (n + 1) * INPROJ_COLS)
        o_ref[0, :, out_cols] = _dot(h, src[:, c:c + INPROJ_COLS]).astype(o_ref.dtype)
        for piece in pieces[n * len(pieces) // len(srcs):(n + 1) * len(pieces) // len(srcs)]:
            piece()
    ba_ref[0] = _dot(h, wba_ref[...])


def _inproj(x, gains, mod, w, w_gate, w_ba, cos, sin_signed, qk_gains, layer, tm):
    b, s, d = x.shape
    n, n_gate = 3 * DA_WIDTH + 4 * DN_WIDTH, w_gate.shape[2]
    assert w.shape[2] >= n
    nh = DA_HEADS
    n_store = COL_STORED * LANES
    assert n_gate % INPROJ_COLS == 0 and (n - MAIN_DN * LANES) % INPROJ_COLS == 0
    assert n_store == n_gate + n - MAIN_DN * LANES
    resident = dict(pipeline_mode=pl.Buffered(1))
    bf16 = jnp.bfloat16
    return pl.pallas_call(
        _inproj_kernel,
        grid=(b, s // tm),
        in_specs=[
            pl.BlockSpec((1, tm, d), lambda bi, i: (bi, i, 0)),
            _layer_spec((1, d), layer),
            _mod_spec(d, layer, MOD_SCALE1),
            _mod_spec(d, layer, MOD_SHIFT1),
            _layer_spec((d, n), layer, **resident),
            _layer_spec((d, n_gate), layer, **resident),
            _layer_spec((d, LANES), layer, **resident),
            pl.BlockSpec((tm, LANES), lambda bi, i: (i, 0)),
            pl.BlockSpec((tm, LANES), lambda bi, i: (i, 0)),
            pl.BlockSpec((None, None, 1, LANES), lambda bi, i: (layer, 0, 0, 0)),
            pl.BlockSpec((None, None, 1, LANES), lambda bi, i: (layer, 1, 0, 0)),
        ],
        out_specs=[pl.BlockSpec((1, tm, n_store), lambda bi, i: (bi, i, 0)),
                   pl.BlockSpec((1, tm, LANES), lambda bi, i: (bi, i, 0)),
                   pl.BlockSpec((1, nh, 2, tm, LANES), lambda bi, i: (bi, 0, 0, i, 0)),
                   pl.BlockSpec((1, nh, LANES, tm), lambda bi, i: (bi, 0, 0, i)),
                   pl.BlockSpec((1, nh, tm, 2 * LANES), lambda bi, i: (bi, 0, i, 0))],
        out_shape=[jax.ShapeDtypeStruct((b, s, n_store), bf16),
                   jax.ShapeDtypeStruct((b, s, LANES), jnp.float32),
                   jax.ShapeDtypeStruct((b, nh, 2, s, LANES), bf16),
                   jax.ShapeDtypeStruct((b, nh, LANES, s), bf16),
                   jax.ShapeDtypeStruct((b, nh, s, 2 * LANES), bf16)],
        compiler_params=_params("parallel", "parallel"),
        name="inproj",
    )(x, gains, mod, mod, w, w_gate, w_ba, cos, sin_signed, qk_gains, qk_gains)


ATTN_UNROLL = 8
ATTN_ROW_BLOCK = 32


def _attn_kernel(q_ref, kt_ref, v_ref, lam_ref, g_ref, o_ref, s_ref, p_ref, m_ref, alpha_ref, acc_ref,
                 *, tq, tk, lambda_init):
    s_len = v_ref.shape[2]
    q = q_ref[0, 0].reshape(2 * tq, LANES)
    n_kv = s_len // tk
    groups = tk // LANES
    rb = ATTN_ROW_BLOCK

    m_ref[...] = jnp.full(m_ref.shape, -jnp.inf, jnp.float32)
    acc_ref[...] = jnp.zeros(acc_ref.shape, jnp.float32)

    def scores(j, slot):
        s_ref[slot] = _dot(q, kt_ref[0, 0, :, pl.ds(pl.multiple_of(j * tk, tk), tk)])

    def update(j, slot):
        for r in range(2 * tq // rb):
            rows = slice(r * rb, (r + 1) * rb)
            sg = [s_ref[slot, rows, g * LANES:(g + 1) * LANES] for g in range(groups)]
            m_old = m_ref[rows, :]
            m_new = jnp.maximum(m_old, jnp.max(functools.reduce(jnp.maximum, sg), axis=-1, keepdims=True))
            alpha_ref[rows, :] = jnp.exp2(m_old - m_new)
            m_ref[rows, :] = m_new
            for g in range(groups):
                p_ref[rows, g * LANES:(g + 1) * LANES] = jnp.exp2(sg[g] - m_new).astype(jnp.bfloat16)
        alpha = alpha_ref[...]
        pv = _dot(p_ref[...], v_ref[0, 0, pl.ds(pl.multiple_of(j * tk, tk), tk), :])
        acc_ref[...] = jnp.concatenate([alpha, alpha], axis=1) * acc_ref[...] + pv

    def body(jj, carry):
        j = ATTN_UNROLL * jj
        for k in range(ATTN_UNROLL):
            scores(jnp.minimum(j + k + 1, n_kv - 1), (k + 1) % 2)
            update(j + k, k % 2)
        return carry

    scores(0, 0)
    lax.fori_loop(0, n_kv // ATTN_UNROLL, body, 0)
    o = acc_ref[:, :LANES] / acc_ref[:, LANES:]
    lq = lam_ref[...]
    lam = (jnp.exp(jnp.sum(lq[0:1] * lq[1:2], axis=-1, keepdims=True))
           - jnp.exp(jnp.sum(lq[2:3] * lq[3:4], axis=-1, keepdims=True)) + lambda_init)
    d = o[:tq] - lam * o[tq:]
    y = d * lax.rsqrt(jnp.mean(d * d, axis=-1, keepdims=True) + EPS) * g_ref[...]
    o_ref[0] = (y * (1.0 - lambda_init)).astype(o_ref.dtype)


def _attn(qs, kt, v, lam_params, subln_gain, layer, lambda_init, tq, tk):
    b, h, _, s, _ = qs.shape
    assert s % (tk * ATTN_UNROLL) == 0 and ATTN_UNROLL % 2 == 0 and (2 * tq) % ATTN_ROW_BLOCK == 0
    return pl.pallas_call(
        functools.partial(_attn_kernel, tq=tq, tk=tk, lambda_init=lambda_init),
        grid=(b, h, s // tq),
        in_specs=[
            pl.BlockSpec((1, 1, 2, tq, LANES), lambda bi, hi, i: (bi, hi, 0, i, 0)),
            pl.BlockSpec((1, 1, LANES, s), lambda bi, hi, i: (bi, hi, 0, 0)),
            pl.BlockSpec((1, 1, s, 2 * LANES), lambda bi, hi, i: (bi, hi, 0, 0)),
            _layer_spec(lam_params.shape[1:], layer),
            _layer_spec((1, LANES), layer),
        ],
        out_specs=pl.BlockSpec((1, tq, LANES), lambda bi, hi, i: (bi, i, hi)),
        out_shape=jax.ShapeDtypeStruct((b, s, h * LANES), jnp.bfloat16),
        scratch_shapes=[pltpu.VMEM((2, 2 * tq, tk), jnp.float32), pltpu.VMEM((2 * tq, tk), jnp.bfloat16),
                        pltpu.VMEM((2 * tq, LANES), jnp.float32), pltpu.VMEM((2 * tq, LANES), jnp.float32),
                        pltpu.VMEM((2 * tq, 2 * LANES), jnp.float32)],
        compiler_params=_params("parallel", "parallel", "parallel"),
        name="attn",
    )(qs, kt, v, lam_params, subln_gain)


INV_BLOCK = 16
HALO = 16
GDN_UNROLL = 8


def _gdn_prep_kernel(q_ref, qp_ref, qn_ref, k_ref, kp_ref, kn_ref, v_ref, vp_ref, vn_ref,
                     cw_ref, ba_ref, alog_ref, dtb_ref,
                     wq_ref, u_ref, qkkd_ref, gl_ref,
                     qs_ref, ks_ref, vs_ref, bf_ref, bb_ref, gf_ref, gb_ref, xq_ref, xk_ref, xv_ref,
                     *, t, nblk):
    i = pl.program_id(1)
    pad = CONV_K // 2
    bf = jnp.bfloat16

    def conv_silu(h, x_ref, p_ref, n_ref, w, xp_ref):
        f32 = jnp.float32
        cols = slice(h * LANES, (h + 1) * LANES)
        xp_ref[h, 0:HALO, :] = jnp.where(i == 0, 0.0, p_ref[0, :, cols].astype(f32))
        xp_ref[h, HALO:HALO + t, :] = x_ref[0, :, cols].astype(f32)
        xp_ref[h, HALO + t:2 * HALO + t, :] = jnp.where(i == nblk - 1, 0.0, n_ref[0, :, cols].astype(f32))
        acc = xp_ref[h, HALO - pad:HALO - pad + t, :] * w[0:1, cols]
        for j in range(1, CONV_K):
            acc = acc + xp_ref[h, HALO - pad + j:HALO - pad + j + t, :] * w[j:j + 1, cols]
        return _silu(acc)

    def l2n(x):
        return x * lax.rsqrt(jnp.sum(x * x, axis=-1, keepdims=True) + EPS)

    cw = cw_ref[...]
    lane = lax.broadcasted_iota(jnp.int32, (1, LANES), 1)

    r3 = lax.broadcasted_iota(jnp.int32, (CHUNK, 3 * CHUNK), 0)
    c3 = lax.broadcasted_iota(jnp.int32, (CHUNK, 3 * CHUNK), 1) % CHUNK
    tril3 = (r3 >= c3).astype(bf)
    triu3 = (r3 <= c3).astype(bf)

    ba = ba_ref[0]
    g_all = -jnp.exp(alog_ref[...]) * _softplus(ba + dtb_ref[...])
    beta_all = jax.nn.sigmoid(ba)
    hi, mid, lo = _split3(g_all)
    gcf, gcb = [], []
    for c in range(t // CHUNK):
        rows = slice(c * CHUNK, (c + 1) * CHUNK)
        parts = jnp.concatenate([hi[rows], mid[rows], lo[rows]], axis=0).astype(bf)
        gcf.append(_dot(tril3, parts))
        gcb.append(_dot(triu3, parts))

    def pick(x, idx):
        col = jnp.sum(jnp.where(lane == idx, x, 0.0), axis=-1, keepdims=True)
        return jnp.broadcast_to(col, (t, LANES))

    gcf = jnp.concatenate(gcf, axis=0)
    gcb = jnp.concatenate(gcb, axis=0)

    row = lax.broadcasted_iota(jnp.int32, (CHUNK, 2 * CHUNK), 0)
    col2 = lax.broadcasted_iota(jnp.int32, (CHUNK, 2 * CHUNK), 1)
    left = col2 < CHUNK
    colm = col2 % CHUNK
    keep2 = (left & (row >= colm)) | (~left & (row <= colm))
    strict2 = keep2 & (row != colm)
    diag2 = row == colm
    eye2 = diag2.astype(jnp.float32)
    same2 = (row // INV_BLOCK) == (colm // INV_BLOCK)
    row4 = lax.broadcasted_iota(jnp.int32, (2 * CHUNK, 2 * CHUNK), 0)
    col4 = lax.broadcasted_iota(jnp.int32, (2 * CHUNK, 2 * CHUNK), 1)
    bd_mask = ((row4 < CHUNK) == (col4 < CHUNK)).astype(bf)

    def blockdiag(y):
        yb = y.astype(bf)
        return jnp.concatenate([yb, yb], axis=0) * bd_mask

    def st_load(h, c):
        rows = pl.ds(pl.multiple_of(c * CHUNK, CHUNK), CHUNK)
        v = dict(h=h, c=c, rows=rows, qc=qs_ref[h, rows, :], kc=ks_ref[h, rows, :], vc=vs_ref[h, rows, :],
                 beta_f=bf_ref[h, rows, :], beta_b=bb_ref[h, rows, :],
                 gc_f=gf_ref[h, rows, :], gc_b=gb_ref[h, rows, :])
        kcb = v['kc'].astype(bf)
        kk = jnp.concatenate([kcb, kcb], axis=0)
        v['gram2'] = _dot_nt(kcb, kk)
        v['qk2'] = _dot_nt(v['qc'].astype(bf), kk)
        return v

    def st_lmat(v):
        gc2 = jnp.where(left, v['gc_f'], v['gc_b'])
        gc_row = jnp.sum(jnp.where(diag2, gc2, 0.0), axis=0, keepdims=True)
        decay2 = jnp.where(keep2, jnp.exp(jnp.where(keep2, gc2 - gc_row, 0.0)), 0.0)
        l2 = jnp.where(strict2, jnp.where(left, v['beta_f'], v['beta_b']) * v['gram2'] * decay2, 0.0)
        ld = jnp.where(same2, l2, 0.0)
        v.update(qkm=(v['qk2'] * decay2).astype(bf), lo=l2 - ld, p=eye2 - ld,
                 m=_dot(ld.astype(bf), blockdiag(ld)))
        del v['gram2'], v['qk2']
        return v

    def st_level(v):
        r = _dot(jnp.concatenate([v['p'], v['m']], axis=0).astype(bf), blockdiag(v['m']))
        v.update(p=v['p'] + r[:CHUNK], m=r[CHUNK:])
        return v

    def st_dinv(v):
        v['d_inv'] = v['p'] + _dot(v['p'].astype(bf), blockdiag(v['m']))
        return v

    def st_n(v):
        v['n'] = _dot(v['d_inv'].astype(bf), blockdiag(v['lo']))
        return v

    def st_nsq(v):
        v['nsq'] = _dot(v['n'].astype(bf), blockdiag(v['n']))
        return v

    def st_pn(v):
        pn = eye2 - v['n']
        v['pn'] = pn + _dot(pn.astype(bf), blockdiag(v['nsq']))
        return v

    def st_t(v):
        v['t2'] = _dot(v['pn'].astype(bf), blockdiag(v['d_inv']))
        return v

    def st_uw(v):
        kc, vc = v['kc'], v['vc']
        v['egc_f'] = jnp.exp(v['gc_f'])
        v['egc_b'] = jnp.exp(v['gc_b'])
        rhs = jnp.concatenate([
            jnp.concatenate([vc * v['beta_f'], kc * (v['beta_f'] * v['egc_f'])], axis=1),
            jnp.concatenate([vc * v['beta_b'], kc * (v['beta_b'] * v['egc_b'])], axis=1)], axis=0).astype(bf)
        v['uw'] = _dot(blockdiag(v['t2']), rhs)
        return v

    def st_store(v):
        h, c, rows, uw, qc, kc = v['h'], v['c'], v['rows'], v['uw'], v['qc'], v['kc']
        r2 = pl.multiple_of(c * 2 * CHUNK, 2 * CHUNK)
        for d, egc in ((0, v['egc_f']), (1, v['egc_b'])):
            u_ref[d, 0, h, rows, :] = uw[d * CHUNK:(d + 1) * CHUNK, :LANES].astype(bf)
            wq_ref[d, 0, h, pl.ds(r2, CHUNK), :] = uw[d * CHUNK:(d + 1) * CHUNK, LANES:].astype(bf)
            wq_ref[d, 0, h, pl.ds(r2 + CHUNK, CHUNK), :] = (qc * egc).astype(bf)
        gtot_f = v['gc_f'][CHUNK - 1:CHUNK, :]
        gtot_b = v['gc_b'][0:1, :]
        kd = jnp.concatenate([kc * jnp.exp(gtot_f - v['gc_f']), kc * jnp.exp(gtot_b - v['gc_b'])], axis=0)
        r3_ = pl.multiple_of(c * 3 * CHUNK, CHUNK)
        qkkd_ref[0, h, pl.ds(r3_, CHUNK), :] = v['qkm']
        qkkd_ref[0, h, pl.ds(r3_ + CHUNK, 2 * CHUNK), :] = kd.T.astype(bf)
        r8 = pl.multiple_of(c * SUBLANES, SUBLANES)
        gl_ref[0, 0, h, pl.ds(r8, SUBLANES), :] = jnp.broadcast_to(jnp.exp(gtot_f), (SUBLANES, LANES))
        gl_ref[1, 0, h, pl.ds(r8, SUBLANES), :] = jnp.broadcast_to(jnp.exp(gtot_b), (SUBLANES, LANES))

    levels = [st_level] * (int(math.log2(INV_BLOCK)) - 2)
    stages = [st_lmat] + levels + [st_dinv, st_n, st_nsq, st_pn, st_t, st_uw]

    for h in range(DN_HEADS):
        qs_ref[h] = l2n(conv_silu(h, q_ref, qp_ref, qn_ref, cw[0], xq_ref)) * (DN_HEAD_DIM ** -0.5)
        ks_ref[h] = l2n(conv_silu(h, k_ref, kp_ref, kn_ref, cw[1], xk_ref))
        vs_ref[h] = conv_silu(h, v_ref, vp_ref, vn_ref, cw[2], xv_ref)
        bf_ref[h] = pick(beta_all, h)
        bb_ref[h] = pick(beta_all, DN_HEADS + h)
        gf_ref[h] = pick(gcf, 2 * DN_HEADS + h)
        gb_ref[h] = pick(gcb, 3 * DN_HEADS + h)

        def trip(j, carry, h=h):
            vals = [st_load(h, j * GDN_UNROLL + k) for k in range(GDN_UNROLL)]
            for stage in stages:
                vals = [stage(v) for v in vals]
            for v in vals:
                st_store(v)
            return carry

        lax.fori_loop(0, t // (CHUNK * GDN_UNROLL), trip, 0)


def _gdn_prep(proj, ba, conv_w, a_log_row, dt_bias_row, layer, t):
    b, s, _ = proj.shape
    nh = DN_HEADS
    nblk = s // t
    n = s // CHUNK
    tpb = t // HALO
    wcol = DN_WIDTH // LANES

    def main(base):
        return pl.BlockSpec((1, t, DN_WIDTH), lambda bi, i: (bi, i, base // wcol))

    def prev(base):
        return pl.BlockSpec((1, HALO, DN_WIDTH),
                            lambda bi, i: (bi, jnp.maximum(i * tpb - 1, 0), base // wcol))

    def nxt(base):
        return pl.BlockSpec((1, HALO, DN_WIDTH),
                            lambda bi, i: (bi, jnp.minimum((i + 1) * tpb, s // HALO - 1), base // wcol))

    in_specs = []
    for base in (COL_DN_Q, COL_DN_K, COL_DN_V):
        in_specs += [main(base), prev(base), nxt(base)]
    in_specs += [
        _layer_spec((3, CONV_K, DN_WIDTH), layer),
        pl.BlockSpec((1, t, LANES), lambda bi, i: (bi, i, 0)),
        _layer_spec((1, LANES), layer),
        _layer_spec((1, LANES), layer),
    ]
    out_specs = [
        pl.BlockSpec((2, 1, nh, 2 * t, LANES), lambda bi, i: (0, bi, 0, i, 0)),
        pl.BlockSpec((2, 1, nh, t, LANES), lambda bi, i: (0, bi, 0, i, 0)),
        pl.BlockSpec((1, nh, 3 * t, LANES), lambda bi, i: (bi, 0, i, 0)),
        pl.BlockSpec((2, 1, nh, t // CHUNK * SUBLANES, LANES), lambda bi, i: (0, bi, 0, i, 0)),
    ]
    out_shape = [
        jax.ShapeDtypeStruct((2, b, nh, 2 * s, LANES), jnp.bfloat16),
        jax.ShapeDtypeStruct((2, b, nh, s, LANES), jnp.bfloat16),
        jax.ShapeDtypeStruct((b, nh, 3 * s, LANES), jnp.bfloat16),
        jax.ShapeDtypeStruct((2, b, nh, n * SUBLANES, LANES), jnp.float32),
    ]
    proj9 = [proj] * 9
    return pl.pallas_call(
        functools.partial(_gdn_prep_kernel, t=t, nblk=nblk),
        grid=(b, nblk),
        in_specs=in_specs,
        out_specs=out_specs,
        out_shape=out_shape,
        scratch_shapes=([pltpu.VMEM((nh, t, LANES), jnp.float32)] * 7
                        + [pltpu.VMEM((nh, t + 2 * HALO, LANES), jnp.float32)] * 3),
        compiler_params=_params("parallel", "parallel"),
        name="gdn_prep",
    )(*proj9, conv_w, ba, a_log_row, dt_bias_row)


def _gdn_scan_kernel(wq_f, u_f, qkkd_f, gl_f, wq_b, u_b, qkkd_b, gl_b, o_f, o_b, state_ref, *, t):
    nc = t // CHUNK
    nb = u_f.shape[1]

    @pl.when(pl.program_id(0) == 0)
    def _():
        state_ref[...] = jnp.zeros_like(state_ref)

    streams = ((0, wq_f, u_f, qkkd_f, gl_f, o_f), (1, wq_b, u_b, qkkd_b, gl_b, o_b))
    zeros = jnp.zeros((CHUNK, DN_HEAD_DIM), jnp.bfloat16)

    def body(c, carry):
        work = []
        for d, wq, u, qkkd, gl, o in streams:
            cc = c if d == 0 else nc - 1 - c
            r2 = pl.multiple_of(cc * 2 * CHUNK, 2 * CHUNK)
            for bi in range(nb):
                for hh in range(DN_HEADS):
                    st = state_ref[d, bi, hh]
                    ws = _dot(wq[0, bi, hh, pl.ds(r2, 2 * CHUNK), :], st.astype(jnp.bfloat16))
                    work.append((d, bi, hh, cc, u, qkkd, gl, o, st, ws))
        for d, bi, hh, cc, u, qkkd, gl, o, st, ws in work:
            r0 = pl.multiple_of(cc * CHUNK, CHUNK)
            r3 = pl.multiple_of(cc * 3 * CHUNK, CHUNK)
            r8 = pl.multiple_of(cc * SUBLANES, SUBLANES)
            v_new = (u[0, bi, hh, pl.ds(r0, CHUNK), :].astype(jnp.float32) - ws[:CHUNK]).astype(jnp.bfloat16)
            rhs = jnp.concatenate([v_new, zeros] if d == 0 else [zeros, v_new], axis=0)
            r = _dot(qkkd[bi, hh, pl.ds(r3, 3 * CHUNK), :], rhs)
            o[bi, hh, pl.ds(r0, CHUNK), :] = (ws[CHUNK:] + r[:CHUNK]).astype(o.dtype)
            decay = gl[0, bi, hh, pl.ds(r8, SUBLANES), :][0:1]
            state_ref[d, bi, hh] = st * decay + r[CHUNK:]
        return carry

    lax.fori_loop(0, nc, body, 0)


def _gdn_scan(wq, u, qkkd, gl, t):
    _, b, nh, s, _ = u.shape
    nblk = s // t
    g8 = t // CHUNK * SUBLANES

    def specs(d):
        blk = (lambda i: i) if d == 0 else (lambda i: nblk - 1 - i)
        idx = lambda i: (d, 0, 0, blk(i), 0)
        idx4 = lambda i: (0, 0, blk(i), 0)
        return [
            pl.BlockSpec((1, b, nh, 2 * t, LANES), idx),
            pl.BlockSpec((1, b, nh, t, LANES), idx),
            pl.BlockSpec((b, nh, 3 * t, LANES), idx4),
            pl.BlockSpec((1, b, nh, g8, LANES), idx),
        ], pl.BlockSpec((b, nh, t, LANES), idx4)

    in_f, out_f = specs(0)
    in_b, out_b = specs(1)
    o_shape = jax.ShapeDtypeStruct((b, nh, s, LANES), jnp.bfloat16)
    return pl.pallas_call(
        functools.partial(_gdn_scan_kernel, t=t),
        grid=(nblk,),
        in_specs=in_f + in_b,
        out_specs=[out_f, out_b],
        out_shape=[o_shape, o_shape],
        scratch_shapes=[pltpu.VMEM((2, b, nh, DN_HEAD_DIM, DN_HEAD_DIM), jnp.float32)],
        compiler_params=_params("arbitrary"),
        name="gdn_scan",
    )(wq, u, qkkd, gl, wq, u, qkkd, gl)


def _merge_kernel(of_ref, ob_ref, z_ref, ga_ref, gb_ref, ya_ref, x_ref, gt_ref, og_ref,
                  wa_ref, wb_ref, wo_ref, o_ref):
    og = og_ref[...]
    z = z_ref[0].astype(jnp.float32)
    parts = []
    for hh in range(DN_HEADS):
        o = of_ref[0, hh].astype(jnp.float32) + ob_ref[0, hh].astype(jnp.float32)
        y = o * lax.rsqrt(jnp.mean(o * o, axis=-1, keepdims=True) + EPS) * og
        parts.append((y * _silu(z[:, hh * LANES:(hh + 1) * LANES])).astype(jnp.bfloat16))
    yb = jnp.concatenate(parts, axis=1)
    pa = _dot(ya_ref[0], wa_ref[...])
    pb = _dot(yb, wb_ref[...])
    merged = (jax.nn.sigmoid(ga_ref[0].astype(jnp.float32)) * pa
              + jax.nn.sigmoid(gb_ref[0].astype(jnp.float32)) * pb)
    o_ref[0] = x_ref[0] + gt_ref[...] * _dot(merged.astype(jnp.bfloat16), wo_ref[...])


def _merge(o_f, o_b, proj, y_a, x, mod, out_gain, w_a, w_b, w_o, layer, tm):
    b, s, d = x.shape
    nh = DN_HEADS
    zw = DN_WIDTH // LANES
    return pl.pallas_call(
        _merge_kernel,
        grid=(b, s // tm),
        in_specs=[
            pl.BlockSpec((1, nh, tm, LANES), lambda bi, i: (bi, 0, i, 0)),
            pl.BlockSpec((1, nh, tm, LANES), lambda bi, i: (bi, 0, i, 0)),
            pl.BlockSpec((1, tm, DN_WIDTH), lambda bi, i: (bi, i, COL_DN_Z // zw)),
            pl.BlockSpec((1, tm, d), lambda bi, i: (bi, i, 0)),
            pl.BlockSpec((1, tm, d), lambda bi, i: (bi, i, 1)),
            pl.BlockSpec((1, tm, DA_WIDTH), lambda bi, i: (bi, i, 0)),
            pl.BlockSpec((1, tm, d), lambda bi, i: (bi, i, 0)),
            _mod_spec(d, layer, MOD_GATE1),
            _layer_spec((1, LANES), layer),
            _layer_spec(w_a.shape[1:], layer),
            _layer_spec(w_b.shape[1:], layer),
            _layer_spec(w_o.shape[1:], layer),
        ],
        out_specs=pl.BlockSpec((1, tm, d), lambda bi, i: (bi, i, 0)),
        out_shape=jax.ShapeDtypeStruct((b, s, d), jnp.float32),
        compiler_params=_params("parallel", "parallel"),
        name="merge",
    )(o_f, o_b, proj, proj, proj, y_a, x, mod, out_gain, w_a, w_b, w_o)


def _mlp_kernel(x_ref, g_ref, sc_ref, sh_ref, gt_ref, w1_ref, w2_ref, o_ref, h_ref, acc_ref):
    j = pl.program_id(2)

    @pl.when(j == 0)
    def _():
        h_ref[...] = _modulated_norm(x_ref[0], g_ref[...], sc_ref[...], sh_ref[...]).astype(h_ref.dtype)
        acc_ref[...] = jnp.zeros_like(acc_ref)

    a = jnp.maximum(_dot(h_ref[...], w1_ref[...]), 0.0)
    acc_ref[...] += _dot((a * a).astype(jnp.bfloat16), w2_ref[...])

    @pl.when(j == pl.num_programs(2) - 1)
    def _():
        o_ref[0] = x_ref[0] + gt_ref[...] * acc_ref[...]


def _mlp(x, gains, mod, w1, w2, layer, tm, tf):
    b, s, d = x.shape
    f = w1.shape[2]
    return pl.pallas_call(
        _mlp_kernel,
        grid=(b, s // tm, f // tf),
        in_specs=[
            pl.BlockSpec((1, tm, d), lambda bi, i, j: (bi, i, 0)),
            _layer_spec((1, d), layer),
            _mod_spec(d, layer, MOD_SCALE2),
            _mod_spec(d, layer, MOD_SHIFT2),
            _mod_spec(d, layer, MOD_GATE2),
            pl.BlockSpec((None, d, tf), lambda bi, i, j: (layer, 0, j)),
            pl.BlockSpec((None, tf, d), lambda bi, i, j: (layer, j, 0)),
        ],
        out_specs=pl.BlockSpec((1, tm, d), lambda bi, i, j: (bi, i, 0)),
        out_shape=jax.ShapeDtypeStruct((b, s, d), jnp.float32),
        scratch_shapes=[pltpu.VMEM((tm, d), jnp.bfloat16), pltpu.VMEM((tm, d), jnp.float32)],
        compiler_params=_params("parallel", "parallel", "arbitrary"),
        name="mlp",
    )(x, gains, mod, mod, mod, w1, w2)


def _split_w_in(w_in, d_model):
    assert d_model == D_MODEL
    main = 3 * DA_WIDTH + 4 * DN_WIDTH
    small = 4 * DN_HEADS
    w16 = w_in.astype(jnp.bfloat16)
    pad = jnp.zeros(w16.shape[:-1] + (LANES - small,), w16.dtype)
    return w16, w16[..., main + small:], jnp.concatenate([w16[..., main:main + small], pad], axis=-1)


def _rope_tables(seq):
    pos = jnp.arange(seq, dtype=jnp.float32)
    inv = 1.0 / (ROPE_THETA ** (jnp.arange(0, DA_HEAD_DIM, 2, dtype=jnp.float32) / DA_HEAD_DIM))
    ang = pos[:, None] * inv[None, :]
    reps = LANES // ang.shape[1]
    sign = jnp.where((jnp.arange(LANES) % DA_HEAD_DIM) < DA_HEAD_DIM // 2, -1.0, 1.0)
    return jnp.tile(jnp.cos(ang), (1, reps)), jnp.tile(jnp.sin(ang), (1, reps)) * sign


class _Tiles(NamedTuple):
    proj_rows: int
    mlp_rows: int
    mlp_cols: int
    gdn_rows: int
    attn_q: int
    attn_k: int


def _tile_sizes(s):
    return _Tiles(proj_rows=min(512, s), mlp_rows=min(1024, s), mlp_cols=1024,
                  gdn_rows=min(CHUNK * GDN_UNROLL, s), attn_q=min(512, s), attn_k=min(512, s // ATTN_UNROLL))


def kernel(x, c, ada_w, ada_b, norm1_g, norm2_g, w_in, dn_conv, dn_A_log, dn_dt_bias, dn_out_gain,
           qk_norm_gain, diff_lambda, diff_subln_gain, w_branch_a, w_branch_b, w_out, w_mlp1, w_mlp2):
    b, s, d = x.shape
    depth = ada_w.shape[0]
    assert s % 512 == 0 and d % LANES == 0 and b <= SUBLANES
    tiles = _tile_sizes(s)

    c_pad = jnp.zeros((SUBLANES, d), jnp.float32).at[:b].set(c)
    mod = _adaln(c_pad, ada_w, ada_b)[:, :b].reshape(depth, b, 6, 1, d)
    cos, sin_signed = _rope_tables(s)

    bf16 = jnp.bfloat16
    small = 4 * DN_HEADS
    w_main, w_gate, w_ba = _split_w_in(w_in, d)
    norm1, norm2 = norm1_g.reshape(depth, 1, d), norm2_g.reshape(depth, 1, d)
    qk_gains = jnp.tile(qk_norm_gain, (1, 1, 2)).reshape(depth, 2, 1, LANES)
    subln_gain = diff_subln_gain.reshape(depth, 1, LANES)
    out_gain = dn_out_gain.reshape(depth, 1, LANES)
    conv_w = dn_conv.reshape(depth, CONV_K, 3, DN_WIDTH).transpose(0, 2, 1, 3)
    rows = jnp.zeros((depth, 1, LANES), jnp.float32)
    a_log_rows = rows.at[:, 0, small // 2:small].set(dn_A_log.reshape(depth, -1))
    dt_bias_rows = rows.at[:, 0, small // 2:small].set(dn_dt_bias.reshape(depth, -1))
    w_a, w_b, w_o = w_branch_a.astype(bf16), w_branch_b.astype(bf16), w_out.astype(bf16)
    w1, w2 = w_mlp1.astype(bf16), w_mlp2.astype(bf16)

    for l in range(depth):
        lambda_init = 0.8 - 0.6 * math.exp(-0.3 * l)
        proj, ba, qs, kt, v = _inproj(x, norm1, mod, w_main, w_gate, w_ba, cos, sin_signed, qk_gains, l,
                                      tiles.proj_rows)
        y_a = _attn(qs, kt, v, diff_lambda, subln_gain, l, lambda_init, tiles.attn_q, tiles.attn_k)
        wq, u, qkkd, gl = _gdn_prep(proj, ba, conv_w, a_log_rows, dt_bias_rows, l, tiles.gdn_rows)
        o_f, o_b = _gdn_scan(wq, u, qkkd, gl, tiles.gdn_rows)
        x = _merge(o_f, o_b, proj, y_a, x, mod, out_gain, w_a, w_b, w_o, l, tiles.proj_rows)
        x = _mlp(x, norm2, mod, w1, w2, l, tiles.mlp_rows, tiles.mlp_cols)
    return x
```
